```python
import math
import jax, jax.numpy as jnp
from jax import lax
import numpy as np

D_MODEL = 1024
BATCH = 16
SEQ = 4096
DEPTH = 4

LRU_WIDTH = D_MODEL // 4
LRU_BLOCKS = 4
LRU_BLOCK = LRU_WIDTH // LRU_BLOCKS
CONV_WIDTH = 4
LRU_C = 8.0
MLSTM_HEADS = 4
MLSTM_HEAD_DIM = D_MODEL // 8
MLSTM_WIDTH = MLSTM_HEADS * MLSTM_HEAD_DIM
MLSTM_CHUNK = 64
GATE_SOFTCAP = 15.0
ATTN_HEADS = 4
ATTN_HEAD_DIM = D_MODEL // 16
ATTN_WIDTH = ATTN_HEADS * ATTN_HEAD_DIM
Q_LORA_RANK = D_MODEL // 4
KV_LORA_RANK = D_MODEL // 8
IDX_HEADS = 8
IDX_DIM = D_MODEL // 32
INDEX_TOPK = 256
Q_BLOCK = 128
REL_BUCKETS = 32
REL_MAX_EXACT = 16
REL_MAX_DIST = 128
D_FF = 4 * D_MODEL
N_ADA = 6
NORM_EPS = 1e-6
IN_SPLITS = (LRU_WIDTH, LRU_WIDTH,
             MLSTM_WIDTH, MLSTM_WIDTH, MLSTM_WIDTH, MLSTM_WIDTH,
             MLSTM_HEADS, MLSTM_HEADS,
             Q_LORA_RANK, KV_LORA_RANK, IDX_DIM, IDX_HEADS)
N_IN = sum(IN_SPLITS)

kernel_name = 'hymba_lru_mlstm_dsa_block'


def rmsnorm(x, w):
    xf = x.astype(jnp.float32)
    y = xf * lax.rsqrt(jnp.mean(xf * xf, axis=-1, keepdims=True) + NORM_EPS)
    return (y * w.astype(jnp.float32)).astype(x.dtype)


def softcap(z):
    return GATE_SOFTCAP * jnp.tanh(z / GATE_SOFTCAP)


def t5_bucket(dist):
    n = jnp.maximum(dist, 0)
    log_ratio = jnp.log(jnp.maximum(n, 1).astype(jnp.float32) / REL_MAX_EXACT) / math.log(REL_MAX_DIST / REL_MAX_EXACT)
    large = REL_MAX_EXACT + (log_ratio * (REL_BUCKETS - REL_MAX_EXACT)).astype(jnp.int32)
    large = jnp.minimum(large, REL_BUCKETS - 1)
    return jnp.where(n < REL_MAX_EXACT, n, large)


def rg_lru_group(xb, yb, conv_w, conv_b, wa, ba, wx, bx, lam):
    B, S, C = xb.shape
    xc = lax.conv_general_dilated(xb, conv_w[:, None, :], window_strides=(1,),
                                  padding=[(CONV_WIDTH - 1, 0)],
                                  dimension_numbers=('NWC', 'WIO', 'NWC'),
                                  feature_group_count=C) + conv_b
    xblk = xc.reshape(B, S, LRU_BLOCKS, LRU_BLOCK)
    r = jax.nn.sigmoid(jnp.einsum('bsni,nij->bsnj', xblk, wa) + ba).reshape(B, S, C)
    gi = jax.nn.sigmoid(jnp.einsum('bsni,nij->bsnj', xblk, wx) + bx).reshape(B, S, C)
    log_a = -LRU_C * r.astype(jnp.float32) * jax.nn.softplus(-lam.astype(jnp.float32))
    a = jnp.exp(log_a)
    u = jnp.sqrt(-jnp.expm1(2.0 * log_a)) * (gi * xc).astype(jnp.float32)

    def combine(left, right):
        a1, b1 = left
        a2, b2 = right
        return a1 * a2, a2 * b1 + b2

    _, h = lax.associative_scan(combine, (a, u), axis=1)
    return jax.nn.gelu(yb) * h.astype(yb.dtype)


def mlstm_chunkwise(q, k, v, i_g, f_g):
    B, S, H, Dh = q.shape
    nc = S // MLSTM_CHUNK
    q = q.astype(jnp.float32) * (Dh ** -0.5)
    k = k.astype(jnp.float32)
    v = v.astype(jnp.float32)
    logf = jax.nn.log_sigmoid(f_g.astype(jnp.float32))
    ig = i_g.astype(jnp.float32)

    def to_chunks(t):
        return t.reshape(B, nc, MLSTM_CHUNK, H, -1).transpose(1, 0, 3, 2, 4)

    def gate_chunks(t):
        return t.reshape(B, nc, MLSTM_CHUNK, H).transpose(1, 0, 3, 2)

    causal = jnp.tril(jnp.ones((MLSTM_CHUNK, MLSTM_CHUNK), dtype=bool))

    def step(carry, inp):
        C, n, m = carry
        qc, kc, vc, ic, lf = inp
        b = jnp.cumsum(lf, axis=-1)
        D = b[..., :, None] - b[..., None, :] + ic[..., None, :]
        D = jnp.where(causal, D, -jnp.inf)
        inter = b + m[..., None]
        m_row = jnp.maximum(inter, jnp.max(D, axis=-1))
        inter_w = jnp.exp(inter - m_row)
        s = jnp.einsum('bhjd,bhtd->bhjt', qc, kc) * jnp.exp(D - m_row[..., None])
        num = jnp.einsum('bhjt,bhtv->bhjv', s, vc) + inter_w[..., None] * jnp.einsum('bhjk,bhkv->bhjv', qc, C)
        den = jnp.sum(s, axis=-1) + inter_w * jnp.einsum('bhjk,bhk->bhj', qc, n)
        h = num / jnp.maximum(jnp.abs(den), jnp.exp(-m_row))[..., None]
        b_last = b[..., -1]
        w_state = b_last[..., None] - b + ic
        m_new = jnp.maximum(b_last + m, jnp.max(w_state, axis=-1))
        decay = jnp.exp(b_last + m - m_new)
        wexp = jnp.exp(w_state - m_new[..., None])
        C_new = decay[..., None, None] * C + jnp.einsum('bht,bhtk,bhtv->bhkv', wexp, kc, vc)
        n_new = decay[..., None] * n + jnp.einsum('bht,bhtk->bhk', wexp, kc)
        return (C_new, n_new, m_new), h

    init = (jnp.zeros((B, H, Dh, Dh), jnp.float32), jnp.zeros((B, H, Dh), jnp.float32),
            jnp.zeros((B, H), jnp.float32))
    _, hs = lax.scan(step, init, (to_chunks(q), to_chunks(k), to_chunks(v), gate_chunks(ig), gate_chunks(logf)))
    return hs.transpose(1, 0, 3, 2, 4).reshape(B, S, H, Dh)


def mlstm_group(m_q, m_k, m_v, m_o, m_i, m_f, bi, bf, norm_w):
    B, S, _ = m_q.shape
    shp = (B, S, MLSTM_HEADS, MLSTM_HEAD_DIM)
    i_g = softcap(m_i.astype(jnp.float32) + bi)
    f_g = softcap(m_f.astype(jnp.float32) + bf)
    h = mlstm_chunkwise(m_q.reshape(shp), m_k.reshape(shp), m_v.reshape(shp), i_g, f_g)
    h = rmsnorm(h, norm_w.reshape(MLSTM_HEADS, MLSTM_HEAD_DIM))
    return jax.nn.sigmoid(m_o) * h.reshape(B, S, MLSTM_WIDTH).astype(m_o.dtype)


def dsa_group(a_q, a_kv, i_k, i_w, w_q_up, w_qidx_up, w_uk, w_uv, q_norm_w, kv_norm_w, rel_bias):
    B, S, _ = a_q.shape
    q_lat = rmsnorm(a_q, q_norm_w)
    c_kv = rmsnorm(a_kv, kv_norm_w)
    q = jnp.einsum('bsr,rhd->bshd', q_lat, w_q_up)
    q_abs = jnp.einsum('bshd,chd->bshc', q, w_uk) * (ATTN_HEAD_DIM ** -0.5)
    q_idx = jnp.einsum('bsr,rhd->bshd', q_lat, w_qidx_up) * (IDX_DIM ** -0.5)
    w_i = i_w * (IDX_HEADS ** -0.5)
    topk = min(INDEX_TOPK, S // 4)
    nblk = S // Q_BLOCK
    s_pos = jnp.arange(S)

    def block(bidx):
        start = bidx * Q_BLOCK
        t_pos = start + jnp.arange(Q_BLOCK)
        qi = lax.dynamic_slice_in_dim(q_idx, start, Q_BLOCK, axis=1)
        wi = lax.dynamic_slice_in_dim(w_i, start, Q_BLOCK, axis=1)
        qa = lax.dynamic_slice_in_dim(q_abs, start, Q_BLOCK, axis=1)
        score = jnp.einsum('bths,bth->bts', jax.nn.relu(jnp.einsum('bthd,bsd->bths', qi, i_k)), wi).astype(jnp.float32)
        score = jnp.where((s_pos[None, :] <= t_pos[:, None])[None], score, -jnp.inf)
        _, sel = lax.top_k(score, topk)
        valid = sel <= t_pos[None, :, None]
        c_sel = jax.vmap(lambda cb, ib: cb[ib])(c_kv, sel)
        logits = jnp.einsum('bthr,btkr->bthk', qa, c_sel).astype(jnp.float32)
        bias = rel_bias[t5_bucket(t_pos[None, :, None] - sel)]
        logits = logits + bias.transpose(0, 1, 3, 2).astype(jnp.float32)
        logits = jnp.where(valid[:, :, None, :], logits, -jnp.inf)
        p = jax.nn.softmax(logits, axis=-1)
        o_lat = jnp.einsum('bthk,btkr->bthr', p, c_sel)
        return jnp.einsum('bthr,rhd->bthd', o_lat, w_uv).astype(a_q.dtype)

    out = lax.map(block, jnp.arange(nblk))
    return out.transpose(1, 0, 2, 3, 4).reshape(B, S, ATTN_WIDTH)


def hybrid_mixer(h, w_in, conv_w, conv_b, lru_wa, lru_ba, lru_wx, lru_bx, lru_lambda,
                 mlstm_bi, mlstm_bf, w_q_up, w_qidx_up, w_uk, w_uv, q_lat_norm_w, kv_lat_norm_w,
                 rel_bias, group_norm_w, w_o):
    proj = h @ w_in
    split_points = np.cumsum(IN_SPLITS)[:-1].tolist()
    (lru_x, lru_y, m_q, m_k, m_v, m_o, m_i, m_f, a_q, a_kv, i_k, i_w) = jnp.split(proj, split_points, axis=-1)
    g_lru, g_m, g_a = jnp.split(group_norm_w, [LRU_WIDTH, LRU_WIDTH + MLSTM_WIDTH])
    y_lru = rmsnorm(rg_lru_group(lru_x, lru_y, conv_w, conv_b, lru_wa, lru_ba, lru_wx, lru_bx, lru_lambda), g_lru)
    y_m = mlstm_group(m_q, m_k, m_v, m_o, m_i, m_f, mlstm_bi, mlstm_bf, g_m)
    y_a = rmsnorm(dsa_group(a_q, a_kv, i_k, i_w, w_q_up, w_qidx_up, w_uk, w_uv,
                            q_lat_norm_w, kv_lat_norm_w, rel_bias), g_a)
    return jnp.concatenate([y_lru, y_m.astype(y_lru.dtype), y_a], axis=-1) @ w_o


def setup_inputs(seed: int = 0) -> dict:
    key = jax.random.key(seed)
    ks = jax.random.split(key, 32)

    def nrm(k, shape, scale):
        return jax.random.normal(k, shape, jnp.float32) * scale

    u = jax.random.uniform(ks[9], (DEPTH, LRU_WIDTH), jnp.float32, minval=0.9, maxval=0.999)
    s = u ** (1.0 / LRU_C)
    return {
        'x': nrm(ks[0], (BATCH, SEQ, D_MODEL), 1.0),
        'c': nrm(ks[1], (BATCH, D_MODEL), 1.0),
        'w_in': nrm(ks[2], (DEPTH, D_MODEL, N_IN), D_MODEL ** -0.5),
        'conv_w': nrm(ks[3], (DEPTH, CONV_WIDTH, LRU_WIDTH), CONV_WIDTH ** -0.5),
        'conv_b': nrm(ks[4], (DEPTH, LRU_WIDTH), 0.01),
        'lru_wa': nrm(ks[5], (DEPTH, LRU_BLOCKS, LRU_BLOCK, LRU_BLOCK), LRU_BLOCK ** -0.5),
        'lru_ba': nrm(ks[6], (DEPTH, LRU_BLOCKS, LRU_BLOCK), 0.01),
        'lru_wx': nrm(ks[7], (DEPTH, LRU_BLOCKS, LRU_BLOCK, LRU_BLOCK), LRU_BLOCK ** -0.5),
        'lru_bx': nrm(ks[8], (DEPTH, LRU_BLOCKS, LRU_BLOCK), 0.01),
        'lru_lambda': jnp.log(s) - jnp.log1p(-s),
        'mlstm_bi': nrm(ks[10], (DEPTH, MLSTM_HEADS), 0.1),
        'mlstm_bf': jnp.linspace(3.0, 6.0, MLSTM_HEADS, dtype=jnp.float32)[None, :] + nrm(ks[11], (DEPTH, MLSTM_HEADS), 0.1),
        'w_q_up': nrm(ks[12], (DEPTH, Q_LORA_RANK, ATTN_HEADS, ATTN_HEAD_DIM), Q_LORA_RANK ** -0.5),
        'w_qidx_up': nrm(ks[13], (DEPTH, Q_LORA_RANK, IDX_HEADS, IDX_DIM), Q_LORA_RANK ** -0.5),
        'w_uk': nrm(ks[14], (DEPTH, KV_LORA_RANK, ATTN_HEADS, ATTN_HEAD_DIM), KV_LORA_RANK ** -0.5),
        'w_uv': nrm(ks[15], (DEPTH, KV_LORA_RANK, ATTN_HEADS, ATTN_HEAD_DIM), KV_LORA_RANK ** -0.5),
        'q_lat_norm_w': 1.0 + nrm(ks[16], (DEPTH, Q_LORA_RANK), 0.02),
        'kv_lat_norm_w': 1.0 + nrm(ks[17], (DEPTH, KV_LORA_RANK), 0.02),
        'rel_bias': nrm(ks[18], (REL_BUCKETS, ATTN_HEADS), 0.5),
        'group_norm_w': 1.0 + nrm(ks[19], (DEPTH, D_MODEL), 0.02),
        'w_o': nrm(ks[20], (DEPTH, D_MODEL, D_MODEL), D_MODEL ** -0.5),
        'w_ada': nrm(ks[21], (DEPTH, D_MODEL, N_ADA * D_MODEL), 0.5 * D_MODEL ** -0.5),
        'b_ada': nrm(ks[22], (DEPTH, N_ADA * D_MODEL), 0.02),
        'norm1_w': 1.0 + nrm(ks[23], (DEPTH, D_MODEL), 0.02),
        'norm2_w': 1.0 + nrm(ks[24], (DEPTH, D_MODEL), 0.02),
        'w_mlp1': nrm(ks[25], (DEPTH, D_MODEL, D_FF), D_MODEL ** -0.5),
        'w_mlp2': nrm(ks[26], (DEPTH, D_FF, D_MODEL), D_FF ** -0.5),
        'final_norm_w': 1.0 + nrm(ks[27], (D_MODEL,), 0.02),
    }


def reference(x, c, w_in, conv_w, conv_b, lru_wa, lru_ba, lru_wx, lru_bx, lru_lambda,
              mlstm_bi, mlstm_bf, w_q_up, w_qidx_up, w_uk, w_uv, q_lat_norm_w, kv_lat_norm_w,
              rel_bias, group_norm_w, w_o, w_ada, b_ada, norm1_w, norm2_w, w_mlp1, w_mlp2,
              final_norm_w):
    c_act = jax.nn.silu(c)
    for l in range(DEPTH):
        mod = c_act @ w_ada[l] + b_ada[l]
        sh1, sc1, g1, sh2, sc2, g2 = jnp.split(mod[:, None, :], N_ADA, axis=-1)
        h = rmsnorm(x, norm1_w[l]) * (1.0 + sc1) + sh1
        mix = hybrid_mixer(h, w_in[l], conv_w[l], conv_b[l], lru_wa[l], lru_ba[l], lru_wx[l], lru_bx[l],
                           lru_lambda[l], mlstm_bi[l], mlstm_bf[l], w_q_up[l], w_qidx_up[l], w_uk[l], w_uv[l],
                           q_lat_norm_w[l], kv_lat_norm_w[l], rel_bias, group_norm_w[l], w_o[l])
        x = x + g1 * mix
        h = rmsnorm(x, norm2_w[l]) * (1.0 + sc2) + sh2
        x = x + g2 * (jnp.square(jax.nn.relu(h @ w_mlp1[l])) @ w_mlp2[l])
    return rmsnorm(x, final_norm_w)
```

```python
import functools
import math

import jax
import jax.numpy as jnp
from jax import lax
from jax.experimental import pallas as pl
from jax.experimental.pallas import tpu as pltpu

F32 = jnp.float32
BF16 = jnp.bfloat16
I32 = jnp.int32

NORM_EPS = 1e-6
LRU_BLOCKS = 4
CONV_WIDTH = 4
LRU_C = 8.0
MLSTM_HEADS = 4
GATE_SOFTCAP = 15.0
ATTN_HEADS = 4
IDX_HEADS = 8
INDEX_TOPK = 256
REL_BUCKETS = 32
REL_MAX_EXACT = 16
REL_MAX_DIST = 128
N_ADA = 6

LANE = 128
Q_BLOCK = 128
KV_CHUNK = 512
MLSTM_CHUNK = 256
TOKEN_TILE = 512
FF_CHUNK = 1024
INT_MIN = -(2 ** 31)
INT_MAX = 2 ** 31 - 1
NEG_BIG = -1e30
M_INIT = -1e29
MIB = 1024 * 1024


def _rms(x, w):
    return x * lax.rsqrt(jnp.mean(x * x, axis=-1, keepdims=True) + NORM_EPS) * w


def _sigmoid(x):
    return 1.0 / (1.0 + jnp.exp(-x))


def _dot(a, b):
    return jnp.dot(a, b, preferred_element_type=F32)


def _dot_nt(a, b):
    return lax.dot_general(a, b, (((1,), (1,)), ((), ())), preferred_element_type=F32)


def _params(sem, vmem_mib):
    return pltpu.CompilerParams(dimension_semantics=sem, vmem_limit_bytes=vmem_mib * MIB)


def _ada_kernel(c_ref, w_ref, b_ref, o_ref):
    c = c_ref[...]
    act = (c * _sigmoid(c)).astype(BF16)
    o_ref[0, 0] = _dot(act, w_ref[0].astype(BF16)) + b_ref[0, 0]


def _ada(c, w_ada, b_ada):
    depth, d, _ = w_ada.shape
    bsz = c.shape[0]
    b4 = b_ada.reshape(depth, N_ADA, 1, d)
    return pl.pallas_call(
        _ada_kernel,
        grid=(depth, N_ADA),
        in_specs=[
            pl.BlockSpec((bsz, d), lambda l, k: (0, 0)),
            pl.BlockSpec((1, d, d), lambda l, k: (l, 0, k)),
            pl.BlockSpec((1, 1, 1, d), lambda l, k: (l, k, 0, 0)),
        ],
        out_specs=pl.BlockSpec((1, 1, bsz, d), lambda l, k: (l, k, 0, 0)),
        out_shape=jax.ShapeDtypeStruct((depth, N_ADA, bsz, d), F32),
        compiler_params=_params(("arbitrary", "arbitrary"), 32),
        name="ada_mod",
    )(c, w_ada, b4)


def _inproj_kernel(x_ref, mod_ref, nw_ref, w_ref, lru_ref, qkv_ref, mo_ref, aq_ref, akv_ref, misc_ref, *, dims):
    cl, cm, cq, ckv = dims
    x = x_ref[0]
    sh = mod_ref[0, 0]
    sc = mod_ref[1, 0]
    h = _rms(x, nw_ref[...]) * (1.0 + sc) + sh
    proj = _dot(h.astype(BF16), w_ref[...])
    o = 0
    lru_ref[0] = proj[:, o:o + 2 * cl]
    o += 2 * cl
    qkv_ref[0] = proj[:, o:o + 3 * cm].astype(BF16)
    o += 3 * cm
    mo_ref[0] = proj[:, o:o + cm]
    o += cm
    aq_ref[0] = proj[:, o:o + cq]
    o += cq
    akv_ref[0] = proj[:, o:o + ckv]
    o += ckv
    misc_ref[0] = proj[:, o:o + LANE]


def _inproj(x, mod, nw, w_in_p, dims):
    bsz, s, d = x.shape
    cl, cm, cq, ckv = dims
    n_out = w_in_p.shape[1]
    t = min(TOKEN_TILE, s)
    tok = lambda c: pl.BlockSpec((1, t, c), lambda b, i: (b, i, 0))
    return pl.pallas_call(
        functools.partial(_inproj_kernel, dims=dims),
        grid=(bsz, s // t),
        in_specs=[
            tok(d),
            pl.BlockSpec((N_ADA, 1, 1, d), lambda b, i: (0, b, 0, 0)),
            pl.BlockSpec((1, d), lambda b, i: (0, 0)),
            pl.BlockSpec((d, n_out), lambda b, i: (0, 0)),
        ],
        out_specs=[tok(2 * cl), tok(3 * cm), tok(cm), tok(cq), tok(ckv), tok(LANE)],
        out_shape=[
            jax.ShapeDtypeStruct((bsz, s, 2 * cl), F32),
            jax.ShapeDtypeStruct((bsz, s, 3 * cm), BF16),
            jax.ShapeDtypeStruct((bsz, s, cm), F32),
            jax.ShapeDtypeStruct((bsz, s, cq), F32),
            jax.ShapeDtypeStruct((bsz, s, ckv), F32),
            jax.ShapeDtypeStruct((bsz, s, LANE), F32),
        ],
        compiler_params=_params(("parallel", "parallel"), 56),
        name="in_proj",
    )(x, mod, nw, w_in_p)


def _lru_kernel(p_ref, cw_ref, cb_ref, wa_ref, ba_ref, wx_ref, bx_ref, lam_ref, g_ref, out_ref, hc_ref, xt_ref):
    t_len = p_ref.shape[1]
    c = p_ref.shape[2] // 2

    @pl.when(pl.program_id(1) == 0)
    def _():
        hc_ref[...] = jnp.zeros_like(hc_ref)
        xt_ref[...] = jnp.zeros_like(xt_ref)

    p = p_ref[0]
    x = p[:, :c]
    y = p[:, c:]
    cw = cw_ref[...]
    xe = jnp.concatenate([xt_ref[...], x], axis=0)
    xc = cb_ref[...] + cw[CONV_WIDTH - 1:CONV_WIDTH] * x
    for j in range(1, CONV_WIDTH):
        xc = xc + cw[CONV_WIDTH - 1 - j:CONV_WIDTH - j] * pltpu.roll(xe, j, axis=0)[8:]
    xt_ref[...] = x[t_len - 8:]

    xb = xc.astype(BF16)
    r = _sigmoid(_dot(xb, wa_ref[...]) + ba_ref[...])
    gi = _sigmoid(_dot(xb, wx_ref[...]) + bx_ref[...])
    nl = -lam_ref[...]
    softplus = jnp.maximum(nl, 0.0) + jnp.log(1.0 + jnp.exp(-jnp.abs(nl)))
    log_a = -LRU_C * r * softplus
    a = jnp.exp(log_a)
    u = jnp.sqrt(1.0 - jnp.exp(2.0 * log_a)) * (gi * xc)

    row = lax.broadcasted_iota(I32, (t_len, c), 0)
    d = 1
    while d < t_len:
        keep = row >= d
        a_s = jnp.where(keep, pltpu.roll(a, d, axis=0), 1.0)
        u_s = jnp.where(keep, pltpu.roll(u, d, axis=0), 0.0)
        u = a * u_s + u
        a = a * a_s
        d *= 2
    h = a * hc_ref[...] + u
    hc_ref[...] = h[t_len - 1:]

    gelu = 0.5 * y * (1.0 + jnp.tanh(math.sqrt(2.0 / math.pi) * (y + 0.044715 * (y * y * y))))
    out_ref[0] = _rms(gelu * h, g_ref[...]).astype(BF16)


def _lru(p_lru, conv_w, conv_b, wa_bd, ba, wx_bd, bx, lam, g_lru):
    bsz, s, c2 = p_lru.shape
    c = c2 // 2
    t = min(TOKEN_TILE, s)
    row = lambda n: pl.BlockSpec((1, n), lambda b, i: (0, 0))
    full = lambda a: pl.BlockSpec(a.shape, lambda b, i: (0, 0))
    return pl.pallas_call(
        _lru_kernel,
        grid=(bsz, s // t),
        in_specs=[
            pl.BlockSpec((1, t, c2), lambda b, i: (b, i, 0)),
            full(conv_w), row(c), full(wa_bd), row(c), full(wx_bd), row(c), row(c), row(c),
        ],
        out_specs=pl.BlockSpec((1, t, c), lambda b, i: (b, i, 0)),
        out_shape=jax.ShapeDtypeStruct((bsz, s, c), BF16),
        scratch_shapes=[pltpu.VMEM((1, c), F32), pltpu.VMEM((8, c), F32)],
        compiler_params=_params(("arbitrary", "arbitrary"), 32),
        name="rg_lru",
    )(p_lru, conv_w, conv_b, wa_bd, ba, wx_bd, bx, lam, g_lru)


def _mlstm_kernel(qkv_ref, mo_ref, misc_ref, gb_ref, nw_ref, out_ref, c_ref, n_ref, m_ref):
    l_len = qkv_ref.shape[1]
    dh = c_ref.shape[1]
    nh = c_ref.shape[0]

    @pl.when(pl.program_id(1) == 0)
    def _():
        c_ref[...] = jnp.zeros_like(c_ref)
        n_ref[...] = jnp.zeros_like(n_ref)
        m_ref[...] = jnp.zeros_like(m_ref)

    g = misc_ref[0] + gb_ref[...]
    capped = GATE_SOFTCAP * jnp.tanh(g / GATE_SOFTCAP)
    logf = jnp.minimum(capped, 0.0) - jnp.log(1.0 + jnp.exp(-jnp.abs(capped)))
    row = lax.broadcasted_iota(I32, (l_len, LANE), 0)
    bcum = logf
    d = 1
    while d < l_len:
        bcum = bcum + jnp.where(row >= d, pltpu.roll(bcum, d, axis=0), 0.0)
        d *= 2
    capped_t = capped.T
    bcum_t = bcum.T
    causal = (lax.broadcasted_iota(I32, (l_len, l_len), 1) <= lax.broadcasted_iota(I32, (l_len, l_len), 0))

    for h in range(nh):
        q = qkv_ref[0, :, h * dh:(h + 1) * dh]
        k = qkv_ref[0, :, (nh + h) * dh:(nh + h + 1) * dh]
        v = qkv_ref[0, :, (2 * nh + h) * dh:(2 * nh + h + 1) * dh]
        qs32 = q.astype(F32) * (dh ** -0.5)
        qs = qs32.astype(BF16)
        b_col = bcum[:, nh + h:nh + h + 1]
        i_col = capped[:, h:h + 1]
        b_row = bcum_t[nh + h:nh + h + 1, :]
        i_row = capped_t[h:h + 1, :]
        m_prev = m_ref[h][:, 0:1]
        n_row = n_ref[h]
        c_old = c_ref[h]

        dmat = jnp.where(causal, b_col - b_row + i_row, -jnp.inf)
        inter = b_col + m_prev
        m_row = jnp.maximum(inter, jnp.max(dmat, axis=1, keepdims=True))
        inter_w = jnp.exp(inter - m_row)
        smat = _dot_nt(qs, k) * jnp.exp(dmat - m_row)
        num = _dot(smat.astype(BF16), v) + inter_w * _dot(qs, c_old.astype(BF16))
        qn = jnp.sum(qs32 * n_row, axis=1, keepdims=True)
        den = jnp.sum(smat, axis=1, keepdims=True) + inter_w * qn
        hh = num / jnp.maximum(jnp.abs(den), jnp.exp(-m_row))

        b_last = b_col[l_len - 1:, :]
        w_col = b_last - b_col + i_col
        m_new = jnp.maximum(b_last + m_prev, jnp.max(w_col, axis=0, keepdims=True))
        decay = jnp.exp(b_last + m_prev - m_new)
        kw = k.astype(F32) * jnp.exp(w_col - m_new)
        c_ref[h] = decay * c_old + _dot(kw.T.astype(BF16), v)
        n_ref[h] = decay * n_row + jnp.sum(kw, axis=0, keepdims=True)
        m_ref[h] = jnp.broadcast_to(m_new, (1, LANE))

        hn = _rms(hh, nw_ref[:, h * dh:(h + 1) * dh])
        gate = _sigmoid(mo_ref[0, :, h * dh:(h + 1) * dh])
        out_ref[0, :, h * dh:(h + 1) * dh] = (gate * hn).astype(BF16)


def _mlstm(p_qkv, p_mo, p_misc, gate_bias, g_m):
    bsz, s, cm = p_mo.shape
    nh = MLSTM_HEADS
    dh = cm // nh
    l_len = min(MLSTM_CHUNK, s)
    tok = lambda c: pl.BlockSpec((1, l_len, c), lambda b, i: (b, i, 0))
    return pl.pallas_call(
        _mlstm_kernel,
        grid=(bsz, s // l_len),
        in_specs=[
            tok(3 * cm), tok(cm), tok(LANE),
            pl.BlockSpec((1, LANE), lambda b, i: (0, 0)),
            pl.BlockSpec((1, cm), lambda b, i: (0, 0)),
        ],
        out_specs=tok(cm),
        out_shape=jax.ShapeDtypeStruct((bsz, s, cm), BF16),
        scratch_shapes=[pltpu.VMEM((nh, dh, dh), F32), pltpu.VMEM((nh, 1, dh), F32), pltpu.VMEM((nh, 1, LANE), F32)],
        compiler_params=_params(("arbitrary", "arbitrary"), 32),
        name="mlstm",
    )(p_qkv, p_mo, p_misc, gate_bias, g_m)


def _dsa_kernel(rb_ref, aq_ref, akv_ref, misck_ref, miscq_ref, qnw_ref, kvnw_ref, wq_ref, wqi_ref, wuk_ref, wuv_ref,
                g_ref, out_ref, ckv_s, ik_s, key_s, nb_s, *, topk):
    qi = pl.program_id(1)
    tq = aq_ref.shape[1]
    n_heads = wuk_ref.shape[0]
    dh = wuk_ref.shape[1]
    d_idx = ik_s.shape[1]

    @pl.when(qi == 0)
    def _():
        ckv_s[...] = _rms(akv_ref[0], kvnw_ref[...]).astype(BF16)
        ik_s[...] = misck_ref[0][:, 32:32 + d_idx].astype(BF16)
        ii = lax.broadcasted_iota(I32, (tq, 2 * tq), 0)
        jj = lax.broadcasted_iota(I32, (tq, 2 * tq), 1)
        n = jnp.maximum(ii + tq - jj, 0)
        log_ratio = jnp.log(jnp.maximum(n, 1).astype(F32) / REL_MAX_EXACT) / math.log(REL_MAX_DIST / REL_MAX_EXACT)
        large = jnp.minimum(REL_MAX_EXACT + (log_ratio * (REL_BUCKETS - REL_MAX_EXACT)).astype(I32), REL_BUCKETS - 1)
        bucket = jnp.where(n < REL_MAX_EXACT, n, large)
        for h in range(n_heads):
            last = rb_ref[REL_BUCKETS - 1, h]
            val = jnp.zeros((tq, 2 * tq), F32)
            for kb in range(REL_BUCKETS - 1):
                val = jnp.where(bucket == kb, rb_ref[kb, h] - last, val)
            nb_s[h] = val

    q_lat = _rms(aq_ref[0], qnw_ref[...]).astype(BF16)
    q = _dot(q_lat, wq_ref[...]).astype(BF16)
    q_idx = (_dot(q_lat, wqi_ref[...]) * (d_idx ** -0.5)).astype(BF16)
    w_i = miscq_ref[0][:, 8:8 + IDX_HEADS] * (IDX_HEADS ** -0.5)

    n_chunks = lax.shift_right_logical(qi + 4, 2)
    t_pos = qi * tq + lax.broadcasted_iota(I32, (tq, KV_CHUNK), 0)
    col = lax.broadcasted_iota(I32, (tq, KV_CHUNK), 1)

    def score_chunk(c, carry):
        k0 = pl.multiple_of(c * KV_CHUNK, KV_CHUNK)
        ik = ik_s[pl.ds(k0, KV_CHUNK), :]
        sc = jnp.zeros((tq, KV_CHUNK), F32)
        for h in range(IDX_HEADS):
            sc = sc + jnp.maximum(_dot_nt(q_idx[:, h * d_idx:(h + 1) * d_idx], ik), 0.0) * w_i[:, h:h + 1]
        bits = lax.bitcast_convert_type(sc, I32)
        key = jnp.where(bits < 0, bits ^ INT_MAX, bits)
        key_s[:, pl.ds(k0, KV_CHUNK)] = jnp.where(col + k0 <= t_pos, key, INT_MIN)
        return carry

    lax.fori_loop(0, n_chunks, score_chunk, 0)

    def bit_step(i, ans):
        cand = ans ^ lax.shift_left(jnp.int32(1), 31 - i)
        cand_b = jnp.broadcast_to(cand, (tq, LANE))

        def count_chunk(c, acc):
            k0 = pl.multiple_of(c * KV_CHUNK, KV_CHUNK)
            blk = key_s[:, pl.ds(k0, KV_CHUNK)]
            for gidx in range(KV_CHUNK // LANE):
                acc = acc + jnp.where(blk[:, gidx * LANE:(gidx + 1) * LANE] >= cand_b, 1, 0)
            return acc

        acc = lax.fori_loop(0, n_chunks, count_chunk, jnp.zeros((tq, LANE), I32))
        cnt = jnp.sum(acc, axis=1, keepdims=True)
        return jnp.where(cnt >= topk, cand, ans)

    thr = lax.fori_loop(0, 32, bit_step, jnp.full((tq, 1), INT_MIN, I32))
    thr = jnp.maximum(thr, INT_MIN + 1)

    prev_blk = jnp.maximum(qi - 1, 0)
    far_limit = prev_blk * tq
    n_far = lax.shift_right_logical(prev_blk + 3, 2)
    thr_prev = jnp.where(qi >= 1, thr, INT_MAX)

    def attend(carry, qa, k0, width, thr_v, bias, limit):
        m_old, l_old, acc = carry
        kv = ckv_s[pl.ds(k0, width), :]
        logits = _dot_nt(qa, kv)
        if bias is not None:
            logits = logits + bias
        keys = key_s[:, pl.ds(k0, width)]
        logits = jnp.where(keys >= thr_v, logits, NEG_BIG)
        if limit is not None:
            cols = lax.broadcasted_iota(I32, (tq, width), 1) + k0
            logits = jnp.where(cols < limit, logits, NEG_BIG)
        m_new = jnp.maximum(m_old, jnp.max(logits, axis=1, keepdims=True))
        alpha = jnp.exp(m_old - m_new)
        p = jnp.exp(logits - m_new)
        l_new = alpha * l_old + jnp.sum(p, axis=1, keepdims=True)
        acc = alpha * acc + _dot(p.astype(BF16), kv)
        return m_new, l_new, acc

    outs = []
    for h in range(n_heads):
        qa = (_dot(q[:, h * dh:(h + 1) * dh], wuk_ref[h]) * (dh ** -0.5)).astype(BF16)
        carry = (jnp.full((tq, 1), M_INIT, F32), jnp.zeros((tq, 1), F32), jnp.zeros((tq, ckv_s.shape[1]), F32))

        def far_chunk(c, carry, qa=qa):
            return attend(carry, qa, pl.multiple_of(c * KV_CHUNK, KV_CHUNK), KV_CHUNK, thr, None, far_limit)

        carry = lax.fori_loop(0, n_far, far_chunk, carry)
        carry = attend(carry, qa, pl.multiple_of(prev_blk * tq, tq), tq, thr_prev, nb_s[h][:, :tq], None)
        carry = attend(carry, qa, pl.multiple_of(qi * tq, tq), tq, thr, nb_s[h][:, tq:], None)
        _, l_fin, acc = carry
        o_lat = (acc / l_fin).astype(BF16)
        outs.append(_dot(o_lat, wuv_ref[h]))
    o = jnp.concatenate(outs, axis=1)
    out_ref[0] = _rms(o, g_ref[...]).astype(BF16)


def _dsa(rel_bias, p_aq, p_akv, p_misc, qnw, kvnw, wq, wqi, wuk_t, wuv_h, g_a):
    bsz, s, cq = p_aq.shape
    ckv = p_akv.shape[2]
    d_idx = wqi.shape[1] // IDX_HEADS
    topk = min(INDEX_TOPK, s // 4)
    tq = Q_BLOCK
    assert s % KV_CHUNK == 0 and s % tq == 0
    cw = wq.shape[1]
    full2 = lambda a: pl.BlockSpec(a.shape, lambda b, i: (0, 0))
    full3 = lambda a: pl.BlockSpec(a.shape, lambda b, i: (0, 0, 0))
    return pl.pallas_call(
        functools.partial(_dsa_kernel, topk=topk),
        grid=(bsz, s // tq),
        in_specs=[
            pl.BlockSpec(memory_space=pltpu.SMEM),
            pl.BlockSpec((1, tq, cq), lambda b, i: (b, i, 0)),
            pl.BlockSpec((1, s, ckv), lambda b, i: (b, 0, 0)),
            pl.BlockSpec((1, s, LANE), lambda b, i: (b, 0, 0)),
            pl.BlockSpec((1, tq, LANE), lambda b, i: (b, i, 0)),
            full2(qnw), full2(kvnw), full2(wq), full2(wqi), full3(wuk_t), full3(wuv_h), full2(g_a),
        ],
        out_specs=pl.BlockSpec((1, tq, cw), lambda b, i: (b, i, 0)),
        out_shape=jax.ShapeDtypeStruct((bsz, s, cw), BF16),
        scratch_shapes=[
            pltpu.VMEM((s, ckv), BF16),
            pltpu.VMEM((s, d_idx), BF16),
            pltpu.VMEM((tq, s), I32),
            pltpu.VMEM((ATTN_HEADS, tq, 2 * tq), F32),
        ],
        compiler_params=_params(("arbitrary", "arbitrary"), 40),
        name="dsa",
    )(rel_bias, p_aq, p_akv, p_misc, p_misc, qnw, kvnw, wq, wqi, wuk_t, wuv_h, g_a)


def _out_mlp_kernel(x_ref, yl_ref, ym_ref, ya_ref, mod_ref, wo_ref, n2_ref, w1_ref, w2_ref, fw_ref, out_ref,
                    x1_s, h2_s, acc_s, *, final):
    j = pl.program_id(2)

    @pl.when(j == 0)
    def _():
        cat = jnp.concatenate([yl_ref[0], ym_ref[0], ya_ref[0]], axis=1)
        x1 = x_ref[0] + mod_ref[2, 0] * _dot(cat, wo_ref[...])
        x1_s[...] = x1
        h2_s[...] = (_rms(x1, n2_ref[...]) * (1.0 + mod_ref[4, 0]) + mod_ref[3, 0]).astype(BF16)
        acc_s[...] = jnp.zeros_like(acc_s)

    a = jnp.maximum(_dot(h2_s[...], w1_ref[...]), 0.0)
    acc_s[...] += _dot((a * a).astype(BF16), w2_ref[...])

    @pl.when(j == pl.num_programs(2) - 1)
    def _():
        x2 = x1_s[...] + mod_ref[5, 0] * acc_s[...]
        if final:
            x2 = _rms(x2, fw_ref[...])
        out_ref[0] = x2


def _out_mlp(x, y_lru, y_m, y_a, mod, wo, n2w, w1, w2, fw, final):
    bsz, s, d = x.shape
    dff = w1.shape[1]
    t = min(TOKEN_TILE, s)
    fc = min(FF_CHUNK, dff)
    tok = lambda c: pl.BlockSpec((1, t, c), lambda b, i, j: (b, i, 0))
    return pl.pallas_call(
        functools.partial(_out_mlp_kernel, final=final),
        grid=(bsz, s // t, dff // fc),
        in_specs=[
            tok(d), tok(y_lru.shape[2]), tok(y_m.shape[2]), tok(y_a.shape[2]),
            pl.BlockSpec((N_ADA, 1, 1, d), lambda b, i, j: (0, b, 0, 0)),
            pl.BlockSpec((d, d), lambda b, i, j: (0, 0)),
            pl.BlockSpec((1, d), lambda b, i, j: (0, 0)),
            pl.BlockSpec((d, fc), lambda b, i, j: (0, j)),
            pl.BlockSpec((fc, d), lambda b, i, j: (j, 0)),
            pl.BlockSpec((1, d), lambda b, i, j: (0, 0)),
        ],
        out_specs=tok(d),
        out_shape=jax.ShapeDtypeStruct((bsz, s, d), F32),
        scratch_shapes=[pltpu.VMEM((t, d), F32), pltpu.VMEM((t, d), BF16), pltpu.VMEM((t, d), F32)],
        compiler_params=_params(("parallel", "parallel", "arbitrary"), 48),
        name="out_mlp",
    )(x, y_lru, y_m, y_a, mod, wo, n2w, w1, w2, fw)


def _block_diag(w):
    nb, bi, bo = w.shape
    out = jnp.zeros((nb * bi, nb * bo), w.dtype)
    for n in range(nb):
        out = out.at[n * bi:(n + 1) * bi, n * bo:(n + 1) * bo].set(w[n])
    return out


def kernel(x, c, w_in, conv_w, conv_b, lru_wa, lru_ba, lru_wx, lru_bx, lru_lambda, mlstm_bi, mlstm_bf, w_q_up,
           w_qidx_up, w_uk, w_uv, q_lat_norm_w, kv_lat_norm_w, rel_bias, group_norm_w, w_o, w_ada, b_ada, norm1_w,
           norm2_w, w_mlp1, w_mlp2, final_norm_w):
    depth, d, _ = w_in.shape
    bsz = x.shape[0]
    cl = conv_w.shape[2]
    cm = d // 2
    cq = w_q_up.shape[1]
    ckv = w_uk.shape[1]
    d_idx = w_qidx_up.shape[3]
    nh = MLSTM_HEADS
    dims = (cl, cm, cq, ckv)

    mod_all = _ada(c, w_ada, b_ada).reshape(depth, N_ADA, bsz, 1, d)

    o_gate = 2 * cl + 4 * cm
    o_aq = o_gate + 2 * nh
    o_akv = o_aq + cq
    o_ik = o_akv + ckv
    o_iw = o_ik + d_idx
    zeros = lambda n: jnp.zeros((depth, d, n), w_in.dtype)
    w_in_p = jnp.concatenate([
        w_in[:, :, :o_gate], w_in[:, :, o_aq:o_akv], w_in[:, :, o_akv:o_ik],
        w_in[:, :, o_gate:o_aq], w_in[:, :, o_iw:o_iw + IDX_HEADS], zeros(32 - 2 * nh - IDX_HEADS),
        w_in[:, :, o_ik:o_iw], zeros(LANE - 32 - d_idx),
    ], axis=2).astype(BF16)

    gate_bias = jnp.concatenate([mlstm_bi, mlstm_bf, jnp.zeros((depth, LANE - 2 * nh), F32)], axis=1)
    fw = final_norm_w.reshape(1, d)

    for l in range(depth):
        mod = mod_all[l]
        p_lru, p_qkv, p_mo, p_aq, p_akv, p_misc = _inproj(x, mod, norm1_w[l].reshape(1, d), w_in_p[l], dims)
        gw = group_norm_w[l]
        y_lru = _lru(p_lru, conv_w[l], conv_b[l].reshape(1, cl),
                     _block_diag(lru_wa[l]).astype(BF16), lru_ba[l].reshape(1, cl),
                     _block_diag(lru_wx[l]).astype(BF16), lru_bx[l].reshape(1, cl),
                     lru_lambda[l].reshape(1, cl), gw[:cl].reshape(1, cl))
        y_m = _mlstm(p_qkv, p_mo, p_misc, gate_bias[l].reshape(1, LANE), gw[cl:cl + cm].reshape(1, cm))
        wq = w_q_up[l].reshape(cq, -1).astype(BF16)
        wqi = w_qidx_up[l].reshape(cq, -1).astype(BF16)
        wuk_t = jnp.transpose(w_uk[l], (1, 2, 0)).astype(BF16)
        wuv_h = jnp.transpose(w_uv[l], (1, 0, 2)).astype(BF16)
        y_a = _dsa(rel_bias, p_aq, p_akv, p_misc, q_lat_norm_w[l].reshape(1, cq), kv_lat_norm_w[l].reshape(1, ckv),
                   wq, wqi, wuk_t, wuv_h, gw[cl + cm:].reshape(1, -1))
        x = _out_mlp(x, y_lru, y_m, y_a, mod, w_o[l].astype(BF16), norm2_w[l].reshape(1, d),
                     w_mlp1[l].astype(BF16), w_mlp2[l].astype(BF16), fw, final=(l == depth - 1))
    return x
```

```python
import functools
import math

import jax
import jax.numpy as jnp
from jax import lax
from jax.experimental import pallas as pl
from jax.experimental.pallas import tpu as pltpu

F32 = jnp.float32
BF16 = jnp.bfloat16
I32 = jnp.int32

NORM_EPS = 1e-6
LRU_BLOCKS = 4
CONV_WIDTH = 4
LRU_C = 8.0
MLSTM_HEADS = 4
GATE_SOFTCAP = 15.0
ATTN_HEADS = 4
IDX_HEADS = 8
INDEX_TOPK = 256
REL_BUCKETS = 32
REL_MAX_EXACT = 16
REL_MAX_DIST = 128
N_ADA = 6

LANE = 128
Q_BLOCK = 128
KV_CHUNK = 512
ATT_CHUNK = 512
WORD_ROWS = 256
MLSTM_CHUNK = 256
TOKEN_TILE = 512
FF_CHUNK = 1024
INT_MIN = -(2 ** 31)
INT_MAX = 2 ** 31 - 1
NEG_BIG = -1e30
M_INIT = -1e29
LOG2E = math.log2(math.e)
MIB = 1024 * 1024


def _rms(x, w):
    return x * lax.rsqrt(jnp.mean(x * x, axis=-1, keepdims=True) + NORM_EPS) * w


def _sigmoid(x):
    return 1.0 / (1.0 + jnp.exp(-x))


def _dot(a, b):
    return jnp.dot(a, b, preferred_element_type=F32)


def _dot_nt(a, b):
    return lax.dot_general(a, b, (((1,), (1,)), ((), ())), preferred_element_type=F32)


def _bit_transpose32(a):
    a = list(a)
    mask = 0x0000FFFF
    j = 16
    while j:
        k = 0
        while k < 32:
            t = (a[k] ^ lax.shift_right_logical(a[k | j], j)) & mask
            a[k] = a[k] ^ t
            a[k | j] = a[k | j] ^ lax.shift_left(t, j)
            k = ((k | j) + 1) & ~j
        j >>= 1
        mask = (mask ^ (mask << j)) & 0xFFFFFFFF
    return a


def _params(sem, vmem_mib):
    return pltpu.CompilerParams(dimension_semantics=sem, vmem_limit_bytes=vmem_mib * MIB)


def _ada_kernel(c_ref, w_ref, b_ref, o_ref):
    c = c_ref[...]
    act = (c * _sigmoid(c)).astype(BF16)
    o_ref[0, 0] = _dot(act, w_ref[0].astype(BF16)) + b_ref[0, 0]


def _ada(c, w_ada, b_ada):
    depth, d, _ = w_ada.shape
    bsz = c.shape[0]
    b4 = b_ada.reshape(depth, N_ADA, 1, d)
    return pl.pallas_call(
        _ada_kernel,
        grid=(depth, N_ADA),
        in_specs=[
            pl.BlockSpec((bsz, d), lambda l, k: (0, 0)),
            pl.BlockSpec((1, d, d), lambda l, k: (l, 0, k)),
            pl.BlockSpec((1, 1, 1, d), lambda l, k: (l, k, 0, 0)),
        ],
        out_specs=pl.BlockSpec((1, 1, bsz, d), lambda l, k: (l, k, 0, 0)),
        out_shape=jax.ShapeDtypeStruct((depth, N_ADA, bsz, d), F32),
        compiler_params=_params(("arbitrary", "arbitrary"), 32),
        name="ada_mod",
    )(c, w_ada, b4)


def _inproj_kernel(x_ref, mod_ref, nw_ref, w_ref, lru_ref, qkv_ref, mo_ref, aq_ref, akv_ref, misc_ref, *, dims):
    cl, cm, cq, ckv = dims
    x = x_ref[0]
    sh = mod_ref[0, 0]
    sc = mod_ref[1, 0]
    h = _rms(x, nw_ref[...]) * (1.0 + sc) + sh
    proj = _dot(h.astype(BF16), w_ref[...])
    o = 0
    lru_ref[0] = proj[:, o:o + 2 * cl]
    o += 2 * cl
    qkv_ref[0] = proj[:, o:o + 3 * cm].astype(BF16)
    o += 3 * cm
    mo_ref[0] = proj[:, o:o + cm]
    o += cm
    aq_ref[0] = proj[:, o:o + cq]
    o += cq
    akv_ref[0] = proj[:, o:o + ckv]
    o += ckv
    misc_ref[0] = proj[:, o:o + LANE]


def _inproj(x, mod, nw, w_in_p, dims):
    bsz, s, d = x.shape
    cl, cm, cq, ckv = dims
    n_out = w_in_p.shape[1]
    t = min(TOKEN_TILE, s)
    tok = lambda c: pl.BlockSpec((1, t, c), lambda b, i: (b, i, 0))
    return pl.pallas_call(
        functools.partial(_inproj_kernel, dims=dims),
        grid=(bsz, s // t),
        in_specs=[
            tok(d),
            pl.BlockSpec((N_ADA, 1, 1, d), lambda b, i: (0, b, 0, 0)),
            pl.BlockSpec((1, d), lambda b, i: (0, 0)),
            pl.BlockSpec((d, n_out), lambda b, i: (0, 0)),
        ],
        out_specs=[tok(2 * cl), tok(3 * cm), tok(cm), tok(cq), tok(ckv), tok(LANE)],
        out_shape=[
            jax.ShapeDtypeStruct((bsz, s, 2 * cl), F32),
            jax.ShapeDtypeStruct((bsz, s, 3 * cm), BF16),
            jax.ShapeDtypeStruct((bsz, s, cm), F32),
            jax.ShapeDtypeStruct((bsz, s, cq), F32),
            jax.ShapeDtypeStruct((bsz, s, ckv), F32),
            jax.ShapeDtypeStruct((bsz, s, LANE), F32),
        ],
        compiler_params=_params(("parallel", "parallel"), 56),
        name="in_proj",
    )(x, mod, nw, w_in_p)


def _lru_kernel(p_ref, cw_ref, cb_ref, wa_ref, ba_ref, wx_ref, bx_ref, lam_ref, g_ref, out_ref, hc_ref, xt_ref):
    t_len = p_ref.shape[1]
    c = p_ref.shape[2] // 2

    @pl.when(pl.program_id(1) == 0)
    def _():
        hc_ref[...] = jnp.zeros_like(hc_ref)
        xt_ref[...] = jnp.zeros_like(xt_ref)

    p = p_ref[0]
    x = p[:, :c]
    y = p[:, c:]
    cw = cw_ref[...]
    xe = jnp.concatenate([xt_ref[...], x], axis=0)
    xc = cb_ref[...] + cw[CONV_WIDTH - 1:CONV_WIDTH] * x
    for j in range(1, CONV_WIDTH):
        xc = xc + cw[CONV_WIDTH - 1 - j:CONV_WIDTH - j] * pltpu.roll(xe, j, axis=0)[8:]
    xt_ref[...] = x[t_len - 8:]

    xb = xc.astype(BF16)
    r = _sigmoid(_dot(xb, wa_ref[...]) + ba_ref[...])
    gi = _sigmoid(_dot(xb, wx_ref[...]) + bx_ref[...])
    nl = -lam_ref[...]
    softplus = jnp.maximum(nl, 0.0) + jnp.log(1.0 + jnp.exp(-jnp.abs(nl)))
    log_a = -LRU_C * r * softplus
    a = jnp.exp(log_a)
    u = jnp.sqrt(1.0 - jnp.exp(2.0 * log_a)) * (gi * xc)

    row = lax.broadcasted_iota(I32, (t_len, c), 0)
    d = 1
    while d < t_len:
        keep = row >= d
        a_s = jnp.where(keep, pltpu.roll(a, d, axis=0), 1.0)
        u_s = jnp.where(keep, pltpu.roll(u, d, axis=0), 0.0)
        u = a * u_s + u
        a = a * a_s
        d *= 2
    h = a * hc_ref[...] + u
    hc_ref[...] = h[t_len - 1:]

    gelu = 0.5 * y * (1.0 + jnp.tanh(math.sqrt(2.0 / math.pi) * (y + 0.044715 * (y * y * y))))
    out_ref[0] = _rms(gelu * h, g_ref[...]).astype(BF16)


def _lru(p_lru, conv_w, conv_b, wa_bd, ba, wx_bd, bx, lam, g_lru):
    bsz, s, c2 = p_lru.shape
    c = c2 // 2
    t = min(TOKEN_TILE, s)
    row = lambda n: pl.BlockSpec((1, n), lambda b, i: (0, 0))
    full = lambda a: pl.BlockSpec(a.shape, lambda b, i: (0, 0))
    return pl.pallas_call(
        _lru_kernel,
        grid=(bsz, s // t),
        in_specs=[
            pl.BlockSpec((1, t, c2), lambda b, i: (b, i, 0)),
            full(conv_w), row(c), full(wa_bd), row(c), full(wx_bd), row(c), row(c), row(c),
        ],
        out_specs=pl.BlockSpec((1, t, c), lambda b, i: (b, i, 0)),
        out_shape=jax.ShapeDtypeStruct((bsz, s, c), BF16),
        scratch_shapes=[pltpu.VMEM((1, c), F32), pltpu.VMEM((8, c), F32)],
        compiler_params=_params(("arbitrary", "arbitrary"), 32),
        name="rg_lru",
    )(p_lru, conv_w, conv_b, wa_bd, ba, wx_bd, bx, lam, g_lru)


def _mlstm_kernel(qkv_ref, mo_ref, misc_ref, gb_ref, nw_ref, out_ref, c_ref, n_ref, m_ref):
    l_len = qkv_ref.shape[1]
    dh = c_ref.shape[1]
    nh = c_ref.shape[0]

    @pl.when(pl.program_id(1) == 0)
    def _():
        c_ref[...] = jnp.zeros_like(c_ref)
        n_ref[...] = jnp.zeros_like(n_ref)
        m_ref[...] = jnp.zeros_like(m_ref)

    g = misc_ref[0] + gb_ref[...]
    capped = GATE_SOFTCAP * jnp.tanh(g / GATE_SOFTCAP)
    logf = jnp.minimum(capped, 0.0) - jnp.log(1.0 + jnp.exp(-jnp.abs(capped)))
    row = lax.broadcasted_iota(I32, (l_len, LANE), 0)
    bcum = logf
    d = 1
    while d < l_len:
        bcum = bcum + jnp.where(row >= d, pltpu.roll(bcum, d, axis=0), 0.0)
        d *= 2
    capped_t = capped.T
    bcum_t = bcum.T
    causal = (lax.broadcasted_iota(I32, (l_len, l_len), 1) <= lax.broadcasted_iota(I32, (l_len, l_len), 0))

    for h in range(nh):
        q = qkv_ref[0, :, h * dh:(h + 1) * dh]
        k = qkv_ref[0, :, (nh + h) * dh:(nh + h + 1) * dh]
        v = qkv_ref[0, :, (2 * nh + h) * dh:(2 * nh + h + 1) * dh]
        qs32 = q.astype(F32) * (dh ** -0.5)
        qs = qs32.astype(BF16)
        b_col = bcum[:, nh + h:nh + h + 1]
        i_col = capped[:, h:h + 1]
        b_row = bcum_t[nh + h:nh + h + 1, :]
        i_row = capped_t[h:h + 1, :]
        m_prev = m_ref[h][:, 0:1]
        n_row = n_ref[h]
        c_old = c_ref[h]

        dmat = jnp.where(causal, b_col - b_row + i_row, -jnp.inf)
        inter = b_col + m_prev
        m_row = jnp.maximum(inter, jnp.max(dmat, axis=1, keepdims=True))
        inter_w = jnp.exp(inter - m_row)
        smat = _dot_nt(qs, k) * jnp.exp(dmat - m_row)
        num = _dot(smat.astype(BF16), v) + inter_w * _dot(qs, c_old.astype(BF16))
        qn = jnp.sum(qs32 * n_row, axis=1, keepdims=True)
        den = jnp.sum(smat, axis=1, keepdims=True) + inter_w * qn
        hh = num / jnp.maximum(jnp.abs(den), jnp.exp(-m_row))

        b_last = b_col[l_len - 1:, :]
        w_col = b_last - b_col + i_col
        m_new = jnp.maximum(b_last + m_prev, jnp.max(w_col, axis=0, keepdims=True))
        decay = jnp.exp(b_last + m_prev - m_new)
        kw = k.astype(F32) * jnp.exp(w_col - m_new)
        c_ref[h] = decay * c_old + _dot(kw.T.astype(BF16), v)
        n_ref[h] = decay * n_row + jnp.sum(kw, axis=0, keepdims=True)
        m_ref[h] = jnp.broadcast_to(m_new, (1, LANE))

        hn = _rms(hh, nw_ref[:, h * dh:(h + 1) * dh])
        gate = _sigmoid(mo_ref[0, :, h * dh:(h + 1) * dh])
        out_ref[0, :, h * dh:(h + 1) * dh] = (gate * hn).astype(BF16)


def _mlstm(p_qkv, p_mo, p_misc, gate_bias, g_m):
    bsz, s, cm = p_mo.shape
    nh = MLSTM_HEADS
    dh = cm // nh
    l_len = min(MLSTM_CHUNK, s)
    tok = lambda c: pl.BlockSpec((1, l_len, c), lambda b, i: (b, i, 0))
    return pl.pallas_call(
        _mlstm_kernel,
        grid=(bsz, s // l_len),
        in_specs=[
            tok(3 * cm), tok(cm), tok(LANE),
            pl.BlockSpec((1, LANE), lambda b, i: (0, 0)),
            pl.BlockSpec((1, cm), lambda b, i: (0, 0)),
        ],
        out_specs=tok(cm),
        out_shape=jax.ShapeDtypeStruct((bsz, s, cm), BF16),
        scratch_shapes=[pltpu.VMEM((nh, dh, dh), F32), pltpu.VMEM((nh, 1, dh), F32), pltpu.VMEM((nh, 1, LANE), F32)],
        compiler_params=_params(("arbitrary", "arbitrary"), 32),
        name="mlstm",
    )(p_qkv, p_mo, p_misc, gate_bias, g_m)


def _dsa_kernel(rb_ref, aq_ref, akv_ref, misck_ref, miscq_ref, qnw_ref, kvnw_ref, wq_ref, wqi_ref, wuk_ref, wuv_ref,
                g_ref, out_ref, ckv_s, ckvt_s, ik_s, key_s, bits_s, nb_s, acc_s, *, topk):
    qi = pl.program_id(1)
    tq = aq_ref.shape[1]
    n_heads = wuk_ref.shape[0]
    dh = wuk_ref.shape[1]
    d_idx = ik_s.shape[1]

    @pl.when(qi == 0)
    def _():
        ckv = _rms(akv_ref[0], kvnw_ref[...])
        ckv_s[...] = ckv.astype(BF16)
        ckvt_s[...] = ckv.T.astype(BF16)
        ik_s[...] = misck_ref[0][:, 32:32 + d_idx].astype(BF16)
        bits_s[...] = jnp.zeros_like(bits_s)
        ii = lax.broadcasted_iota(I32, (2 * tq, tq), 1)
        jj = lax.broadcasted_iota(I32, (2 * tq, tq), 0)
        n = jnp.maximum(ii + tq - jj, 0)
        log_ratio = jnp.log(jnp.maximum(n, 1).astype(F32) / REL_MAX_EXACT) / math.log(REL_MAX_DIST / REL_MAX_EXACT)
        large = jnp.minimum(REL_MAX_EXACT + (log_ratio * (REL_BUCKETS - REL_MAX_EXACT)).astype(I32), REL_BUCKETS - 1)
        bucket = jnp.where(n < REL_MAX_EXACT, n, large)
        for h in range(n_heads):
            last = rb_ref[REL_BUCKETS - 1, h]
            val = jnp.zeros((2 * tq, tq), F32)
            for kb in range(REL_BUCKETS - 1):
                val = jnp.where(bucket == kb, (rb_ref[kb, h] - last) * LOG2E, val)
            nb_s[h] = val

    q_lat = _rms(aq_ref[0], qnw_ref[...]).astype(BF16)
    q = _dot(q_lat, wq_ref[...]).astype(BF16)
    qa_all = jnp.concatenate(
        [(_dot(q[:, h * dh:(h + 1) * dh], wuk_ref[h]) * (dh ** -0.5 * LOG2E)).astype(BF16) for h in range(n_heads)],
        axis=0)
    q_idx_t = (_dot(q_lat, wqi_ref[...]) * (d_idx ** -0.5)).T
    q_idx_all = jnp.concatenate([q_idx_t[h * d_idx:(h + 1) * d_idx] for h in range(IDX_HEADS)], axis=1).astype(BF16)
    w_i = miscq_ref[0].T[8:8 + IDX_HEADS] * (IDX_HEADS ** -0.5)

    n_chunks = lax.shift_right_logical(qi + 4, 2)
    t_pos = qi * tq + lax.broadcasted_iota(I32, (KV_CHUNK, tq), 1)
    row = lax.broadcasted_iota(I32, (KV_CHUNK, tq), 0)

    def score_chunk(c, carry):
        k0 = pl.multiple_of(c * KV_CHUNK, KV_CHUNK)
        d_all = _dot(ik_s[pl.ds(k0, KV_CHUNK), :], q_idx_all)
        sc = jnp.zeros((KV_CHUNK, tq), F32)
        for h in range(IDX_HEADS):
            sc = sc + jnp.maximum(d_all[:, h * tq:(h + 1) * tq], 0.0) * w_i[h:h + 1]
        bits = lax.bitcast_convert_type(sc, I32)
        key = jnp.where(bits < 0, bits ^ INT_MAX, bits)
        key = jnp.where(row + k0 <= t_pos, key, INT_MIN)
        key_s[pl.ds(k0, KV_CHUNK), :] = key
        u = key ^ INT_MIN
        for g in range(KV_CHUNK // WORD_ROWS):
            w = _bit_transpose32([u[g * WORD_ROWS + j * 8:g * WORD_ROWS + (j + 1) * 8] for j in range(32)])
            r0 = pl.multiple_of(c * (KV_CHUNK // 32) + g * 8, 8)
            for i in range(32):
                bits_s[31 - i, pl.ds(r0, 8), :] = w[i]
        return carry

    lax.fori_loop(0, n_chunks, score_chunk, 0)

    n_word_rows = bits_s.shape[1]
    wrow = lax.broadcasted_iota(I32, (n_word_rows, tq), 0)
    first_key = lax.shift_right_logical(wrow, 3) * WORD_ROWS + (wrow & 7)
    t_lane = qi * tq + lax.broadcasted_iota(I32, (n_word_rows, tq), 1)
    n_valid = jnp.clip(lax.shift_right_arithmetic(t_lane - first_key, 3) + 1, 0, 32)
    eq0 = jnp.where(n_valid <= 0, 0, lax.shift_left(jnp.int32(-1), 32 - jnp.maximum(n_valid, 1)))

    def bit_step(i, carry):
        eq, c_gt, ans = carry
        b = 31 - i
        t = eq & bits_s[b]
        cnt = jnp.sum(lax.population_count(t), axis=0, keepdims=True)
        take = (c_gt + cnt) >= topk
        eq = jnp.where(take, t, eq ^ t)
        c_gt = jnp.where(take, c_gt, c_gt + cnt)
        ans = jnp.where(take, ans | lax.shift_left(jnp.int32(1), b), ans)
        return eq, c_gt, ans

    zero_row = jnp.zeros((1, tq), I32)
    _, _, ans = lax.fori_loop(0, 32, bit_step, (eq0, zero_row, zero_row))
    thr = jnp.maximum(ans ^ INT_MIN, INT_MIN + 1)

    prev_blk = jnp.maximum(qi - 1, 0)
    far_limit = prev_blk * tq
    far_blocks = ATT_CHUNK // tq
    n_far = lax.div(prev_blk + (far_blocks - 1), far_blocks)
    thr_prev = jnp.where(qi >= 1, thr, INT_MAX)
    acc_s[...] = jnp.zeros_like(acc_s)

    def attend(carry, k0, width, thr_v, bias_rows, limit):
        m_old, l_old = carry
        kv = ckv_s[pl.ds(k0, width), :]
        kv_t = ckvt_s[:, pl.ds(k0, width)]
        keys = key_s[pl.ds(k0, width), :]
        if limit is not None:
            rows = lax.broadcasted_iota(I32, (width, tq), 0) + k0
            keys = jnp.where(rows < limit, keys, INT_MIN)
        sel = keys >= thr_v
        logits_all = _dot_nt(kv, qa_all)
        m_out, l_out, alphas, ps = [], [], [], []
        for h in range(n_heads):
            logits = logits_all[:, h * tq:(h + 1) * tq]
            if bias_rows is not None:
                logits = logits + nb_s[h, bias_rows[0]:bias_rows[1], :]
            logits = jnp.where(sel, logits, NEG_BIG)
            m_new = jnp.maximum(m_old[h], jnp.max(logits, axis=0, keepdims=True))
            alpha = jnp.exp2(m_old[h] - m_new)
            p = jnp.exp2(logits - m_new)
            l_out.append(alpha * l_old[h] + jnp.sum(p, axis=0, keepdims=True))
            m_out.append(m_new)
            alphas.append(alpha)
            ps.append(p.astype(BF16))
        acc_s[...] = jnp.concatenate(alphas, axis=1) * acc_s[...] + _dot(kv_t, jnp.concatenate(ps, axis=1))
        return tuple(m_out), tuple(l_out)

    carry = (tuple(jnp.full((1, tq), M_INIT, F32) for _ in range(n_heads)),
             tuple(jnp.zeros((1, tq), F32) for _ in range(n_heads)))
    carry = lax.fori_loop(
        0, n_far,
        lambda c, cr: attend(cr, pl.multiple_of(c * ATT_CHUNK, ATT_CHUNK), ATT_CHUNK, thr, None, far_limit), carry)
    carry = attend(carry, pl.multiple_of(prev_blk * tq, tq), tq, thr_prev, (0, tq), None)
    _, l_fin = attend(carry, pl.multiple_of(qi * tq, tq), tq, thr, (tq, 2 * tq), None)
    outs = [_dot(wuv_ref[h], (acc_s[:, h * tq:(h + 1) * tq] / l_fin[h]).astype(BF16)) for h in range(n_heads)]
    o = jnp.concatenate(outs, axis=0).T
    out_ref[0] = _rms(o, g_ref[...]).astype(BF16)


def _dsa(rel_bias, p_aq, p_akv, p_misc, qnw, kvnw, wq, wqi, wuk_t, wuv_h, g_a):
    bsz, s, cq = p_aq.shape
    ckv = p_akv.shape[2]
    d_idx = wqi.shape[1] // IDX_HEADS
    topk = min(INDEX_TOPK, s // 4)
    tq = Q_BLOCK
    assert s % KV_CHUNK == 0 and s % tq == 0
    cw = wq.shape[1]
    full2 = lambda a: pl.BlockSpec(a.shape, lambda b, i: (0, 0))
    full3 = lambda a: pl.BlockSpec(a.shape, lambda b, i: (0, 0, 0))
    return pl.pallas_call(
        functools.partial(_dsa_kernel, topk=topk),
        grid=(bsz, s // tq),
        in_specs=[
            pl.BlockSpec(memory_space=pltpu.SMEM),
            pl.BlockSpec((1, tq, cq), lambda b, i: (b, i, 0)),
            pl.BlockSpec((1, s, ckv), lambda b, i: (b, 0, 0)),
            pl.BlockSpec((1, s, LANE), lambda b, i: (b, 0, 0)),
            pl.BlockSpec((1, tq, LANE), lambda b, i: (b, i, 0)),
            full2(qnw), full2(kvnw), full2(wq), full2(wqi), full3(wuk_t), full3(wuv_h), full2(g_a),
        ],
        out_specs=pl.BlockSpec((1, tq, cw), lambda b, i: (b, i, 0)),
        out_shape=jax.ShapeDtypeStruct((bsz, s, cw), BF16),
        scratch_shapes=[
            pltpu.VMEM((s, ckv), BF16),
            pltpu.VMEM((ckv, s), BF16),
            pltpu.VMEM((s, d_idx), BF16),
            pltpu.VMEM((s, tq), I32),
            pltpu.VMEM((32, s // 32, tq), I32),
            pltpu.VMEM((ATTN_HEADS, 2 * tq, tq), F32),
            pltpu.VMEM((ckv, ATTN_HEADS * tq), F32),
        ],
        compiler_params=_params(("arbitrary", "arbitrary"), 40),
        name="dsa",
    )(rel_bias, p_aq, p_akv, p_misc, p_misc, qnw, kvnw, wq, wqi, wuk_t, wuv_h, g_a)


def _out_mlp_kernel(x_ref, yl_ref, ym_ref, ya_ref, mod_ref, wo_ref, n2_ref, w1_ref, w2_ref, fw_ref, out_ref,
                    x1_s, h2_s, acc_s, *, final):
    j = pl.program_id(2)

    @pl.when(j == 0)
    def _():
        cat = jnp.concatenate([yl_ref[0], ym_ref[0], ya_ref[0]], axis=1)
        x1 = x_ref[0] + mod_ref[2, 0] * _dot(cat, wo_ref[...])
        x1_s[...] = x1
        h2_s[...] = (_rms(x1, n2_ref[...]) * (1.0 + mod_ref[4, 0]) + mod_ref[3, 0]).astype(BF16)
        acc_s[...] = jnp.zeros_like(acc_s)

    a = jnp.maximum(_dot(h2_s[...], w1_ref[...]), 0.0)
    acc_s[...] += _dot((a * a).astype(BF16), w2_ref[...])

    @pl.when(j == pl.num_programs(2) - 1)
    def _():
        x2 = x1_s[...] + mod_ref[5, 0] * acc_s[...]
        if final:
            x2 = _rms(x2, fw_ref[...])
        out_ref[0] = x2


def _out_mlp(x, y_lru, y_m, y_a, mod, wo, n2w, w1, w2, fw, final):
    bsz, s, d = x.shape
    dff = w1.shape[1]
    t = min(TOKEN_TILE, s)
    fc = min(FF_CHUNK, dff)
    tok = lambda c: pl.BlockSpec((1, t, c), lambda b, i, j: (b, i, 0))
    return pl.pallas_call(
        functools.partial(_out_mlp_kernel, final=final),
        grid=(bsz, s // t, dff // fc),
        in_specs=[
            tok(d), tok(y_lru.shape[2]), tok(y_m.shape[2]), tok(y_a.shape[2]),
            pl.BlockSpec((N_ADA, 1, 1, d), lambda b, i, j: (0, b, 0, 0)),
            pl.BlockSpec((d, d), lambda b, i, j: (0, 0)),
            pl.BlockSpec((1, d), lambda b, i, j: (0, 0)),
            pl.BlockSpec((d, fc), lambda b, i, j: (0, j)),
            pl.BlockSpec((fc, d), lambda b, i, j: (j, 0)),
            pl.BlockSpec((1, d), lambda b, i, j: (0, 0)),
        ],
        out_specs=tok(d),
        out_shape=jax.ShapeDtypeStruct((bsz, s, d), F32),
        scratch_shapes=[pltpu.VMEM((t, d), F32), pltpu.VMEM((t, d), BF16), pltpu.VMEM((t, d), F32)],
        compiler_params=_params(("parallel", "parallel", "arbitrary"), 48),
        name="out_mlp",
    )(x, y_lru, y_m, y_a, mod, wo, n2w, w1, w2, fw)


def _block_diag(w):
    nb, bi, bo = w.shape
    out = jnp.zeros((nb * bi, nb * bo), w.dtype)
    for n in range(nb):
        out = out.at[n * bi:(n + 1) * bi, n * bo:(n + 1) * bo].set(w[n])
    return out


def kernel(x, c, w_in, conv_w, conv_b, lru_wa, lru_ba, lru_wx, lru_bx, lru_lambda, mlstm_bi, mlstm_bf, w_q_up,
           w_qidx_up, w_uk, w_uv, q_lat_norm_w, kv_lat_norm_w, rel_bias, group_norm_w, w_o, w_ada, b_ada, norm1_w,
           norm2_w, w_mlp1, w_mlp2, final_norm_w):
    depth, d, _ = w_in.shape
    bsz = x.shape[0]
    cl = conv_w.shape[2]
    cm = d // 2
    cq = w_q_up.shape[1]
    ckv = w_uk.shape[1]
    d_idx = w_qidx_up.shape[3]
    nh = MLSTM_HEADS
    dims = (cl, cm, cq, ckv)

    mod_all = _ada(c, w_ada, b_ada).reshape(depth, N_ADA, bsz, 1, d)

    o_gate = 2 * cl + 4 * cm
    o_aq = o_gate + 2 * nh
    o_akv = o_aq + cq
    o_ik = o_akv + ckv
    o_iw = o_ik + d_idx
    zeros = lambda n: jnp.zeros((depth, d, n), w_in.dtype)
    w_in_p = jnp.concatenate([
        w_in[:, :, :o_gate], w_in[:, :, o_aq:o_akv], w_in[:, :, o_akv:o_ik],
        w_in[:, :, o_gate:o_aq], w_in[:, :, o_iw:o_iw + IDX_HEADS], zeros(32 - 2 * nh - IDX_HEADS),
        w_in[:, :, o_ik:o_iw], zeros(LANE - 32 - d_idx),
    ], axis=2).astype(BF16)

    gate_bias = jnp.concatenate([mlstm_bi, mlstm_bf, jnp.zeros((depth, LANE - 2 * nh), F32)], axis=1)
    fw = final_norm_w.reshape(1, d)

    for l in range(depth):
        mod = mod_all[l]
        p_lru, p_qkv, p_mo, p_aq, p_akv, p_misc = _inproj(x, mod, norm1_w[l].reshape(1, d), w_in_p[l], dims)
        gw = group_norm_w[l]
        y_lru = _lru(p_lru, conv_w[l], conv_b[l].reshape(1, cl),
                     _block_diag(lru_wa[l]).astype(BF16), lru_ba[l].reshape(1, cl),
                     _block_diag(lru_wx[l]).astype(BF16), lru_bx[l].reshape(1, cl),
                     lru_lambda[l].reshape(1, cl), gw[:cl].reshape(1, cl))
        y_m = _mlstm(p_qkv, p_mo, p_misc, gate_bias[l].reshape(1, LANE), gw[cl:cl + cm].reshape(1, cm))
        wq = w_q_up[l].reshape(cq, -1).astype(BF16)
        wqi = w_qidx_up[l].reshape(cq, -1).astype(BF16)
        wuk_t = jnp.transpose(w_uk[l], (1, 2, 0)).astype(BF16)
        wuv_h = jnp.transpose(w_uv[l], (1, 2, 0)).astype(BF16)
        y_a = _dsa(rel_bias, p_aq, p_akv, p_misc, q_lat_norm_w[l].reshape(1, cq), kv_lat_norm_w[l].reshape(1, ckv),
                   wq, wqi, wuk_t, wuv_h, gw[cl + cm:].reshape(1, -1))
        x = _out_mlp(x, y_lru, y_m, y_a, mod, w_o[l].astype(BF16), norm2_w[l].reshape(1, d),
                     w_mlp1[l].astype(BF16), w_mlp2[l].astype(BF16), fw, final=(l == depth - 1))
    return x
```

```python
import functools
import math

import jax
import jax.numpy as jnp
from jax import lax
from jax.experimental import pallas as pl
from jax.experimental.pallas import tpu as pltpu

F32 = jnp.float32
BF16 = jnp.bfloat16
I32 = jnp.int32

NORM_EPS = 1e-6
LRU_BLOCKS = 4
CONV_WIDTH = 4
LRU_C = 8.0
MLSTM_HEADS = 4
GATE_SOFTCAP = 15.0
ATTN_HEADS = 4
IDX_HEADS = 8
INDEX_TOPK = 256
REL_BUCKETS = 32
REL_MAX_EXACT = 16
REL_MAX_DIST = 128
N_ADA = 6

LANE = 128
Q_BLOCK = 256
KV_CHUNK = 2 * Q_BLOCK
BIAS_ROWS = 64
WORD_ROWS = 256
MLSTM_CHUNK = 256
TOKEN_TILE = 512
MLP_TILE = 1024
FF_CHUNK = 1024
INT_MIN = -(2 ** 31)
INT_MAX = 2 ** 31 - 1
NEG_BIG = -1e30
M_INIT = -1e29
LOG2E = math.log2(math.e)
MIB = 1024 * 1024


def _rms(x, w):
    return x * lax.rsqrt(jnp.mean(x * x, axis=-1, keepdims=True) + NORM_EPS) * w


def _sigmoid(x):
    return 1.0 / (1.0 + jnp.exp(-x))


def _dot(a, b):
    return jnp.dot(a, b, preferred_element_type=F32)


def _dot_nt(a, b):
    return lax.dot_general(a, b, (((1,), (1,)), ((), ())), preferred_element_type=F32)


def _bit_transpose32(a):
    a = list(a)
    mask = 0x0000FFFF
    j = 16
    while j:
        k = 0
        while k < 32:
            t = (a[k] ^ lax.shift_right_logical(a[k | j], j)) & mask
            a[k] = a[k] ^ t
            a[k | j] = a[k | j] ^ lax.shift_left(t, j)
            k = ((k | j) + 1) & ~j
        j >>= 1
        mask = (mask ^ (mask << j)) & 0xFFFFFFFF
    return a


def _params(sem, vmem_mib):
    return pltpu.CompilerParams(dimension_semantics=sem, vmem_limit_bytes=vmem_mib * MIB)


def _ada_kernel(c_ref, w_ref, b_ref, o_ref):
    c = c_ref[...]
    act = (c * _sigmoid(c)).astype(BF16)
    o_ref[0, 0] = _dot(act, w_ref[0].astype(BF16)) + b_ref[0, 0]


def _ada(c, w_ada, b_ada):
    depth, d, _ = w_ada.shape
    bsz = c.shape[0]
    b4 = b_ada.reshape(depth, N_ADA, 1, d)
    return pl.pallas_call(
        _ada_kernel,
        grid=(depth, N_ADA),
        in_specs=[
            pl.BlockSpec((bsz, d), lambda l, k: (0, 0)),
            pl.BlockSpec((1, d, d), lambda l, k: (l, 0, k)),
            pl.BlockSpec((1, 1, 1, d), lambda l, k: (l, k, 0, 0)),
        ],
        out_specs=pl.BlockSpec((1, 1, bsz, d), lambda l, k: (l, k, 0, 0)),
        out_shape=jax.ShapeDtypeStruct((depth, N_ADA, bsz, d), F32),
        compiler_params=_params(("arbitrary", "arbitrary"), 32),
        name="ada_mod",
    )(c, w_ada, b4)


def _inproj_kernel(x_ref, mod_ref, nw_ref, w_ref, lru_ref, qkv_ref, mo_ref, aq_ref, akv_ref, misc_ref, *, dims):
    cl, cm, cq, ckv = dims
    x = x_ref[0]
    sh = mod_ref[0, 0]
    sc = mod_ref[1, 0]
    h = _rms(x, nw_ref[...]) * (1.0 + sc) + sh
    proj = _dot(h.astype(BF16), w_ref[...])
    o = 0
    lru_ref[0] = proj[:, o:o + 2 * cl]
    o += 2 * cl
    qkv_ref[0] = proj[:, o:o + 3 * cm].astype(BF16)
    o += 3 * cm
    mo_ref[0] = proj[:, o:o + cm]
    o += cm
    aq_ref[0] = proj[:, o:o + cq]
    o += cq
    akv_ref[0] = proj[:, o:o + ckv]
    o += ckv
    misc_ref[0] = proj[:, o:o + LANE]


def _inproj(x, mod, nw, w_in_p, dims):
    bsz, s, d = x.shape
    cl, cm, cq, ckv = dims
    n_out = w_in_p.shape[1]
    t = min(TOKEN_TILE, s)
    tok = lambda c: pl.BlockSpec((1, t, c), lambda b, i: (b, i, 0))
    return pl.pallas_call(
        functools.partial(_inproj_kernel, dims=dims),
        grid=(bsz, s // t),
        in_specs=[
            tok(d),
            pl.BlockSpec((N_ADA, 1, 1, d), lambda b, i: (0, b, 0, 0)),
            pl.BlockSpec((1, d), lambda b, i: (0, 0)),
            pl.BlockSpec((d, n_out), lambda b, i: (0, 0)),
        ],
        out_specs=[tok(2 * cl), tok(3 * cm), tok(cm), tok(cq), tok(ckv), tok(LANE)],
        out_shape=[
            jax.ShapeDtypeStruct((bsz, s, 2 * cl), F32),
            jax.ShapeDtypeStruct((bsz, s, 3 * cm), BF16),
            jax.ShapeDtypeStruct((bsz, s, cm), F32),
            jax.ShapeDtypeStruct((bsz, s, cq), F32),
            jax.ShapeDtypeStruct((bsz, s, ckv), F32),
            jax.ShapeDtypeStruct((bsz, s, LANE), F32),
        ],
        compiler_params=_params(("parallel", "parallel"), 56),
        name="in_proj",
    )(x, mod, nw, w_in_p)


def _lru_kernel(p_ref, cw_ref, cb_ref, wa_ref, ba_ref, wx_ref, bx_ref, lam_ref, g_ref, out_ref, hc_ref, xt_ref):
    t_len = p_ref.shape[1]
    c = p_ref.shape[2] // 2

    @pl.when(pl.program_id(1) == 0)
    def _():
        hc_ref[...] = jnp.zeros_like(hc_ref)
        xt_ref[...] = jnp.zeros_like(xt_ref)

    p = p_ref[0]
    x = p[:, :c]
    y = p[:, c:]
    cw = cw_ref[...]
    xe = jnp.concatenate([xt_ref[...], x], axis=0)
    xc = cb_ref[...] + cw[CONV_WIDTH - 1:CONV_WIDTH] * x
    for j in range(1, CONV_WIDTH):
        xc = xc + cw[CONV_WIDTH - 1 - j:CONV_WIDTH - j] * pltpu.roll(xe, j, axis=0)[8:]
    xt_ref[...] = x[t_len - 8:]

    xb = xc.astype(BF16)
    r = _sigmoid(_dot(xb, wa_ref[...]) + ba_ref[...])
    gi = _sigmoid(_dot(xb, wx_ref[...]) + bx_ref[...])
    nl = -lam_ref[...]
    softplus = jnp.maximum(nl, 0.0) + jnp.log(1.0 + jnp.exp(-jnp.abs(nl)))
    log_a = -LRU_C * r * softplus
    a = jnp.exp(log_a)
    u = jnp.sqrt(1.0 - jnp.exp(2.0 * log_a)) * (gi * xc)

    row = lax.broadcasted_iota(I32, (t_len, c), 0)
    d = 1
    while d < t_len:
        keep = row >= d
        a_s = jnp.where(keep, pltpu.roll(a, d, axis=0), 1.0)
        u_s = jnp.where(keep, pltpu.roll(u, d, axis=0), 0.0)
        u = a * u_s + u
        a = a * a_s
        d *= 2
    h = a * hc_ref[...] + u
    hc_ref[...] = h[t_len - 1:]

    gelu = 0.5 * y * (1.0 + jnp.tanh(math.sqrt(2.0 / math.pi) * (y + 0.044715 * (y * y * y))))
    out_ref[0] = _rms(gelu * h, g_ref[...]).astype(BF16)


def _lru(p_lru, conv_w, conv_b, wa_bd, ba, wx_bd, bx, lam, g_lru):
    bsz, s, c2 = p_lru.shape
    c = c2 // 2
    t = min(TOKEN_TILE, s)
    row = lambda n: pl.BlockSpec((1, n), lambda b, i: (0, 0))
    full = lambda a: pl.BlockSpec(a.shape, lambda b, i: (0, 0))
    return pl.pallas_call(
        _lru_kernel,
        grid=(bsz, s // t),
        in_specs=[
            pl.BlockSpec((1, t, c2), lambda b, i: (b, i, 0)),
            full(conv_w), row(c), full(wa_bd), row(c), full(wx_bd), row(c), row(c), row(c),
        ],
        out_specs=pl.BlockSpec((1, t, c), lambda b, i: (b, i, 0)),
        out_shape=jax.ShapeDtypeStruct((bsz, s, c), BF16),
        scratch_shapes=[pltpu.VMEM((1, c), F32), pltpu.VMEM((8, c), F32)],
        compiler_params=_params(("arbitrary", "arbitrary"), 32),
        name="rg_lru",
    )(p_lru, conv_w, conv_b, wa_bd, ba, wx_bd, bx, lam, g_lru)


def _mlstm_kernel(qkv_ref, mo_ref, misc_ref, gb_ref, nw_ref, out_ref, ct_ref, m_ref):
    l_len = qkv_ref.shape[1]
    dh = ct_ref.shape[2]
    nh = ct_ref.shape[0]

    @pl.when(pl.program_id(1) == 0)
    def _():
        ct_ref[...] = jnp.zeros_like(ct_ref)
        m_ref[...] = jnp.zeros_like(m_ref)

    g = misc_ref[0] + gb_ref[...]
    capped = GATE_SOFTCAP * jnp.tanh(g / GATE_SOFTCAP)
    logf = jnp.minimum(capped, 0.0) - jnp.log(1.0 + jnp.exp(-jnp.abs(capped)))
    row = lax.broadcasted_iota(I32, (l_len, LANE), 0)
    bcum = logf
    d = 1
    while d < l_len:
        bcum = bcum + jnp.where(row >= d, pltpu.roll(bcum, d, axis=0), 0.0)
        d *= 2
    capped_t = capped.T
    bcum_t = bcum.T
    causal = (lax.broadcasted_iota(I32, (l_len, l_len), 0) <= lax.broadcasted_iota(I32, (l_len, l_len), 1))
    ones_rows = jnp.ones((8, l_len), F32)

    for h in range(nh):
        q = qkv_ref[0, :, h * dh:(h + 1) * dh]
        k = qkv_ref[0, :, (nh + h) * dh:(nh + h + 1) * dh]
        v = qkv_ref[0, :, (2 * nh + h) * dh:(2 * nh + h + 1) * dh]
        qs = (q.astype(F32) * (dh ** -0.5)).astype(BF16)
        b_row = bcum_t[nh + h:nh + h + 1, :]
        i_row = capped_t[h:h + 1, :]
        r_col = capped[:, h:h + 1] - bcum[:, nh + h:nh + h + 1]
        m_prev = m_ref[h][:, 0:1]
        ct_old = ct_ref[h]

        dmat = jnp.where(causal, r_col + b_row, -jnp.inf)
        inter = b_row + m_prev
        m_row = jnp.maximum(inter, jnp.max(dmat, axis=0, keepdims=True))
        inter_w = jnp.exp(inter - m_row)
        smat = _dot_nt(k, qs) * jnp.exp(dmat - m_row)
        v_t = v.astype(F32).T
        cq = _dot_nt(ct_old.astype(BF16), qs)
        num = _dot(v_t.astype(BF16), smat.astype(BF16)) + inter_w * cq[:dh]
        den = jnp.sum(smat, axis=0, keepdims=True) + inter_w * cq[dh:dh + 1]
        hh = num / jnp.maximum(jnp.abs(den), jnp.exp(-m_row))

        b_last = b_row[:, l_len - 1:]
        w_row = b_last - b_row + i_row
        m_new = jnp.maximum(b_last + m_prev, jnp.max(w_row, axis=1, keepdims=True))
        decay = jnp.exp(b_last + m_prev - m_new)
        vw = jnp.concatenate([v_t, ones_rows], axis=0) * jnp.exp(w_row - m_new)
        ct_ref[h] = decay * ct_old + _dot(vw.astype(BF16), k)
        m_ref[h] = jnp.broadcast_to(m_new, (1, LANE))

        hn_t = hh * lax.rsqrt(jnp.mean(hh * hh, axis=0, keepdims=True) + NORM_EPS)
        hn = hn_t.T * nw_ref[:, h * dh:(h + 1) * dh]
        gate = _sigmoid(mo_ref[0, :, h * dh:(h + 1) * dh])
        out_ref[0, :, h * dh:(h + 1) * dh] = (gate * hn).astype(BF16)


def _mlstm(p_qkv, p_mo, p_misc, gate_bias, g_m):
    bsz, s, cm = p_mo.shape
    nh = MLSTM_HEADS
    dh = cm // nh
    l_len = min(MLSTM_CHUNK, s)
    tok = lambda c: pl.BlockSpec((1, l_len, c), lambda b, i: (b, i, 0))
    return pl.pallas_call(
        _mlstm_kernel,
        grid=(bsz, s // l_len),
        in_specs=[
            tok(3 * cm), tok(cm), tok(LANE),
            pl.BlockSpec((1, LANE), lambda b, i: (0, 0)),
            pl.BlockSpec((1, cm), lambda b, i: (0, 0)),
        ],
        out_specs=tok(cm),
        out_shape=jax.ShapeDtypeStruct((bsz, s, cm), BF16),
        scratch_shapes=[pltpu.VMEM((nh, dh + 8, dh), F32), pltpu.VMEM((nh, 1, LANE), F32)],
        compiler_params=_params(("arbitrary", "arbitrary"), 32),
        name="mlstm",
    )(p_qkv, p_mo, p_misc, gate_bias, g_m)


def _dsa_kernel(rb_ref, aq_ref, akv_ref, misck_ref, miscq_ref, qnw_ref, kvnw_ref, wq_ref, wqi_ref, wuk_ref, wuv_ref,
                g_ref, out_ref, ckv_s, ckvt_s, ik_s, key_s, bits_s, nb_s, acc_s, *, topk):
    qi = pl.program_id(1)
    tq = aq_ref.shape[1]
    n_heads = wuk_ref.shape[0]
    dh = wuk_ref.shape[1]
    d_idx = ik_s.shape[1]

    @pl.when(qi == 0)
    def _():
        ckv = _rms(akv_ref[0], kvnw_ref[...])
        ckv_s[...] = ckv.astype(BF16)
        ckvt_s[...] = ckv.T.astype(BF16)
        ik_s[...] = misck_ref[0][:, 32:32 + d_idx].astype(BF16)
        bits_s[...] = jnp.zeros_like(bits_s)
    @pl.when((pl.program_id(0) == 0) & (qi == 0))
    def _():
        def build(rb, carry):
            r0 = pl.multiple_of(rb * BIAS_ROWS, BIAS_ROWS)
            rr = lax.broadcasted_iota(I32, (BIAS_ROWS, tq), 0) + r0
            ii = lax.broadcasted_iota(I32, (BIAS_ROWS, tq), 1)
            n = jnp.maximum(ii + 3 * tq - rr, 0)
            log_ratio = jnp.log(jnp.maximum(n, 1).astype(F32) / REL_MAX_EXACT) / math.log(REL_MAX_DIST / REL_MAX_EXACT)
            large = jnp.minimum(REL_MAX_EXACT + (log_ratio * (REL_BUCKETS - REL_MAX_EXACT)).astype(I32),
                                REL_BUCKETS - 1)
            bucket = jnp.where(n < REL_MAX_EXACT, n, large)
            for h in range(n_heads):
                last = rb_ref[REL_BUCKETS - 1, h]
                val = jnp.zeros((BIAS_ROWS, tq), F32)
                for kb in range(REL_BUCKETS - 1):
                    val = jnp.where(bucket == kb, (rb_ref[kb, h] - last) * LOG2E, val)
                nb_s[h, pl.ds(r0, BIAS_ROWS), :] = val
            return carry

        lax.fori_loop(0, nb_s.shape[1] // BIAS_ROWS, build, 0)


    q_lat = _rms(aq_ref[0], qnw_ref[...]).astype(BF16)
    q = _dot(q_lat, wq_ref[...]).astype(BF16)
    qa_all = jnp.concatenate(
        [(_dot(q[:, h * dh:(h + 1) * dh], wuk_ref[h]) * (dh ** -0.5 * LOG2E)).astype(BF16) for h in range(n_heads)],
        axis=0)
    q_idx_t = (_dot(q_lat, wqi_ref[...]) * (d_idx ** -0.5)).T
    q_idx_all = jnp.concatenate([q_idx_t[h * d_idx:(h + 1) * d_idx] for h in range(IDX_HEADS)], axis=1).astype(BF16)
    w_i = miscq_ref[0].T[8:8 + IDX_HEADS] * (IDX_HEADS ** -0.5)

    n_chunks = lax.div(qi + KV_CHUNK // tq, KV_CHUNK // tq)
    t_pos = qi * tq + lax.broadcasted_iota(I32, (KV_CHUNK, tq), 1)
    row = lax.broadcasted_iota(I32, (KV_CHUNK, tq), 0)

    def score_chunk(c, carry):
        k0 = pl.multiple_of(c * KV_CHUNK, KV_CHUNK)
        d_all = _dot(ik_s[pl.ds(k0, KV_CHUNK), :], q_idx_all)
        sc = jnp.zeros((KV_CHUNK, tq), F32)
        for h in range(IDX_HEADS):
            sc = sc + jnp.maximum(d_all[:, h * tq:(h + 1) * tq], 0.0) * w_i[h:h + 1]
        bits = lax.bitcast_convert_type(sc, I32)
        key = jnp.where(bits < 0, bits ^ INT_MAX, bits)
        key = jnp.where(row + k0 <= t_pos, key, INT_MIN)
        key_s[pl.ds(k0, KV_CHUNK), :] = key
        u = key ^ INT_MIN
        for g in range(KV_CHUNK // WORD_ROWS):
            w = _bit_transpose32([u[g * WORD_ROWS + j * 8:g * WORD_ROWS + (j + 1) * 8] for j in range(32)])
            r0 = pl.multiple_of(c * (KV_CHUNK // 32) + g * 8, 8)
            for i in range(32):
                bits_s[31 - i, pl.ds(r0, 8), :] = w[i]
        return carry

    lax.fori_loop(0, n_chunks, score_chunk, 0)

    n_word_rows = bits_s.shape[1]
    wrow = lax.broadcasted_iota(I32, (n_word_rows, tq), 0)
    first_key = lax.shift_right_logical(wrow, 3) * WORD_ROWS + (wrow & 7)
    t_lane = qi * tq + lax.broadcasted_iota(I32, (n_word_rows, tq), 1)
    n_valid = jnp.clip(lax.shift_right_arithmetic(t_lane - first_key, 3) + 1, 0, 32)
    eq0 = jnp.where(n_valid <= 0, 0, lax.shift_left(jnp.int32(-1), 32 - jnp.maximum(n_valid, 1)))

    def bit_step(i, carry):
        eq, c_gt, ans = carry
        b = 31 - i
        t = eq & bits_s[b]
        cnt = jnp.sum(lax.population_count(t), axis=0, keepdims=True)
        take = (c_gt + cnt) >= topk
        eq = jnp.where(take, t, eq ^ t)
        c_gt = jnp.where(take, c_gt, c_gt + cnt)
        ans = jnp.where(take, ans | lax.shift_left(jnp.int32(1), b), ans)
        return eq, c_gt, ans

    zero_row = jnp.zeros((1, tq), I32)
    _, _, ans = lax.fori_loop(0, 32, bit_step, (eq0, zero_row, zero_row))
    thr = jnp.maximum(ans ^ INT_MIN, INT_MIN + 1)

    c_last = lax.shift_right_logical(qi, 1)
    n_pure = jnp.maximum(c_last - 1, 0)
    acc_s[...] = jnp.zeros_like(acc_s)

    def attend(c, carry, biased):
        m_old, l_old = carry
        k0 = pl.multiple_of(c * KV_CHUNK, KV_CHUNK)
        kv = ckv_s[pl.ds(k0, KV_CHUNK), :]
        kv_t = ckvt_s[:, pl.ds(k0, KV_CHUNK)]
        sel = key_s[pl.ds(k0, KV_CHUNK), :] >= thr
        logits_all = _dot_nt(kv, qa_all)
        if biased:
            b0 = pl.multiple_of((2 * c - qi + 3) * tq, tq)
        m_out, l_out, alphas, ps = [], [], [], []
        for h in range(n_heads):
            logits = logits_all[:, h * tq:(h + 1) * tq]
            if biased:
                logits = logits + nb_s[h, pl.ds(b0, KV_CHUNK), :]
            logits = jnp.where(sel, logits, NEG_BIG)
            m_new = jnp.maximum(m_old[h], jnp.max(logits, axis=0, keepdims=True))
            alpha = jnp.exp2(m_old[h] - m_new)
            p = jnp.exp2(logits - m_new)
            l_out.append(alpha * l_old[h] + jnp.sum(p, axis=0, keepdims=True))
            m_out.append(m_new)
            alphas.append(alpha)
            ps.append(p.astype(BF16))
        acc_s[...] = jnp.concatenate(alphas, axis=1) * acc_s[...] + _dot(kv_t, jnp.concatenate(ps, axis=1))
        return tuple(m_out), tuple(l_out)

    carry = (tuple(jnp.full((1, tq), M_INIT, F32) for _ in range(n_heads)),
             tuple(jnp.zeros((1, tq), F32) for _ in range(n_heads)))
    carry = lax.fori_loop(0, n_pure, functools.partial(attend, biased=False), carry)
    _, l_fin = lax.fori_loop(n_pure, c_last + 1, functools.partial(attend, biased=True), carry)
    outs = [_dot(wuv_ref[h], (acc_s[:, h * tq:(h + 1) * tq] / l_fin[h]).astype(BF16)) for h in range(n_heads)]
    o = jnp.concatenate(outs, axis=0).T
    out_ref[0] = _rms(o, g_ref[...]).astype(BF16)


def _dsa(rel_bias, p_aq, p_akv, p_misc, qnw, kvnw, wq, wqi, wuk_t, wuv_h, g_a):
    bsz, s, cq = p_aq.shape
    ckv = p_akv.shape[2]
    d_idx = wqi.shape[1] // IDX_HEADS
    topk = min(INDEX_TOPK, s // 4)
    tq = Q_BLOCK
    assert s % KV_CHUNK == 0 and s % tq == 0
    cw = wq.shape[1]
    full2 = lambda a: pl.BlockSpec(a.shape, lambda b, i: (0, 0))
    full3 = lambda a: pl.BlockSpec(a.shape, lambda b, i: (0, 0, 0))
    return pl.pallas_call(
        functools.partial(_dsa_kernel, topk=topk),
        grid=(bsz, s // tq),
        in_specs=[
            pl.BlockSpec(memory_space=pltpu.SMEM),
            pl.BlockSpec((1, tq, cq), lambda b, i: (b, i, 0)),
            pl.BlockSpec((1, s, ckv), lambda b, i: (b, 0, 0)),
            pl.BlockSpec((1, s, LANE), lambda b, i: (b, 0, 0)),
            pl.BlockSpec((1, tq, LANE), lambda b, i: (b, i, 0)),
            full2(qnw), full2(kvnw), full2(wq), full2(wqi), full3(wuk_t), full3(wuv_h), full2(g_a),
        ],
        out_specs=pl.BlockSpec((1, tq, cw), lambda b, i: (b, i, 0)),
        out_shape=jax.ShapeDtypeStruct((bsz, s, cw), BF16),
        scratch_shapes=[
            pltpu.VMEM((s, ckv), BF16),
            pltpu.VMEM((ckv, s), BF16),
            pltpu.VMEM((s, d_idx), BF16),
            pltpu.VMEM((s, tq), I32),
            pltpu.VMEM((32, s // 32, tq), I32),
            pltpu.VMEM((ATTN_HEADS, 5 * tq, tq), F32),
            pltpu.VMEM((ckv, ATTN_HEADS * tq), F32),
        ],
        compiler_params=_params(("arbitrary", "arbitrary"), 48),
        name="dsa",
    )(rel_bias, p_aq, p_akv, p_misc, p_misc, qnw, kvnw, wq, wqi, wuk_t, wuv_h, g_a)


def _out_mlp_kernel(x_ref, yl_ref, ym_ref, ya_ref, mod_ref, wo_ref, n2_ref, w1_ref, w2_ref, fw_ref, out_ref,
                    h2_s, acc_s, *, final):
    j = pl.program_id(2)

    @pl.when(j == 0)
    def _():
        cat = jnp.concatenate([yl_ref[0], ym_ref[0], ya_ref[0]], axis=1)
        x1 = x_ref[0] + mod_ref[2, 0] * _dot(cat, wo_ref[...])
        out_ref[0] = x1
        h2_s[...] = (_rms(x1, n2_ref[...]) * (1.0 + mod_ref[4, 0]) + mod_ref[3, 0]).astype(BF16)
        acc_s[...] = jnp.zeros_like(acc_s)

    a = jnp.maximum(_dot(h2_s[...], w1_ref[...]), 0.0)
    acc_s[...] += _dot((a * a).astype(BF16), w2_ref[...])

    @pl.when(j == pl.num_programs(2) - 1)
    def _():
        x2 = out_ref[0] + mod_ref[5, 0] * acc_s[...]
        if final:
            x2 = _rms(x2, fw_ref[...])
        out_ref[0] = x2


def _out_mlp(x, y_lru, y_m, y_a, mod, wo, n2w, w1, w2, fw, final):
    bsz, s, d = x.shape
    dff = w1.shape[1]
    t = min(MLP_TILE, s)
    fc = min(FF_CHUNK, dff)
    tok = lambda c: pl.BlockSpec((1, t, c), lambda b, i, j: (b, i, 0))
    return pl.pallas_call(
        functools.partial(_out_mlp_kernel, final=final),
        grid=(bsz, s // t, dff // fc),
        in_specs=[
            tok(d), tok(y_lru.shape[2]), tok(y_m.shape[2]), tok(y_a.shape[2]),
            pl.BlockSpec((N_ADA, 1, 1, d), lambda b, i, j: (0, b, 0, 0)),
            pl.BlockSpec((d, d), lambda b, i, j: (0, 0)),
            pl.BlockSpec((1, d), lambda b, i, j: (0, 0)),
            pl.BlockSpec((d, fc), lambda b, i, j: (0, j)),
            pl.BlockSpec((fc, d), lambda b, i, j: (j, 0)),
            pl.BlockSpec((1, d), lambda b, i, j: (0, 0)),
        ],
        out_specs=tok(d),
        out_shape=jax.ShapeDtypeStruct((bsz, s, d), F32),
        scratch_shapes=[pltpu.VMEM((t, d), BF16), pltpu.VMEM((t, d), F32)],
        compiler_params=_params(("parallel", "parallel", "arbitrary"), 56),
        name="out_mlp",
    )(x, y_lru, y_m, y_a, mod, wo, n2w, w1, w2, fw)


def _block_diag(w):
    nb, bi, bo = w.shape
    out = jnp.zeros((nb * bi, nb * bo), w.dtype)
    for n in range(nb):
        out = out.at[n * bi:(n + 1) * bi, n * bo:(n + 1) * bo].set(w[n])
    return out


def kernel(x, c, w_in, conv_w, conv_b, lru_wa, lru_ba, lru_wx, lru_bx, lru_lambda, mlstm_bi, mlstm_bf, w_q_up,
           w_qidx_up, w_uk, w_uv, q_lat_norm_w, kv_lat_norm_w, rel_bias, group_norm_w, w_o, w_ada, b_ada, norm1_w,
           norm2_w, w_mlp1, w_mlp2, final_norm_w):
    depth, d, _ = w_in.shape
    bsz = x.shape[0]
    cl = conv_w.shape[2]
    cm = d // 2
    cq = w_q_up.shape[1]
    ckv = w_uk.shape[1]
    d_idx = w_qidx_up.shape[3]
    nh = MLSTM_HEADS
    dims = (cl, cm, cq, ckv)

    mod_all = _ada(c, w_ada, b_ada).reshape(depth, N_ADA, bsz, 1, d)

    o_gate = 2 * cl + 4 * cm
    o_aq = o_gate + 2 * nh
    o_akv = o_aq + cq
    o_ik = o_akv + ckv
    o_iw = o_ik + d_idx
    zeros = lambda n: jnp.zeros((depth, d, n), w_in.dtype)
    w_in_p = jnp.concatenate([
        w_in[:, :, :o_gate], w_in[:, :, o_aq:o_akv], w_in[:, :, o_akv:o_ik],
        w_in[:, :, o_gate:o_aq], w_in[:, :, o_iw:o_iw + IDX_HEADS], zeros(32 - 2 * nh - IDX_HEADS),
        w_in[:, :, o_ik:o_iw], zeros(LANE - 32 - d_idx),
    ], axis=2).astype(BF16)

    gate_bias = jnp.concatenate([mlstm_bi, mlstm_bf, jnp.zeros((depth, LANE - 2 * nh), F32)], axis=1)
    fw = final_norm_w.reshape(1, d)

    for l in range(depth):
        mod = mod_all[l]
        p_lru, p_qkv, p_mo, p_aq, p_akv, p_misc = _inproj(x, mod, norm1_w[l].reshape(1, d), w_in_p[l], dims)
        gw = group_norm_w[l]
        y_lru = _lru(p_lru, conv_w[l], conv_b[l].reshape(1, cl),
                     _block_diag(lru_wa[l]).astype(BF16), lru_ba[l].reshape(1, cl),
                     _block_diag(lru_wx[l]).astype(BF16), lru_bx[l].reshape(1, cl),
                     lru_lambda[l].reshape(1, cl), gw[:cl].reshape(1, cl))
        y_m = _mlstm(p_qkv, p_mo, p_misc, gate_bias[l].reshape(1, LANE), gw[cl:cl + cm].reshape(1, cm))
        wq = w_q_up[l].reshape(cq, -1).astype(BF16)
        wqi = w_qidx_up[l].reshape(cq, -1).astype(BF16)
        wuk_t = jnp.transpose(w_uk[l], (1, 2, 0)).astype(BF16)
        wuv_h = jnp.transpose(w_uv[l], (1, 2, 0)).astype(BF16)
        y_a = _dsa(rel_bias, p_aq, p_akv, p_misc, q_lat_norm_w[l].reshape(1, cq), kv_lat_norm_w[l].reshape(1, ckv),
                   wq, wqi, wuk_t, wuv_h, gw[cl + cm:].reshape(1, -1))
        x = _out_mlp(x, y_lru, y_m, y_a, mod, w_o[l].astype(BF16), norm2_w[l].reshape(1, d),
                     w_mlp1[l].astype(BF16), w_mlp2[l].astype(BF16), fw, final=(l == depth - 1))
    return x
```

```python
import functools
import math

import jax
import jax.numpy as jnp
from jax import lax
from jax.experimental import pallas as pl
from jax.experimental.pallas import tpu as pltpu

F32 = jnp.float32
BF16 = jnp.bfloat16
I32 = jnp.int32

NORM_EPS = 1e-6
LRU_BLOCKS = 4
CONV_WIDTH = 4
LRU_C = 8.0
MLSTM_HEADS = 4
GATE_SOFTCAP = 15.0
ATTN_HEADS = 4
IDX_HEADS = 8
INDEX_TOPK = 256
REL_BUCKETS = 32
REL_MAX_EXACT = 16
REL_MAX_DIST = 128
N_ADA = 6

LANE = 128
Q_BLOCK = 256
KV_CHUNK = 2 * Q_BLOCK
BIAS_ROWS = 64
WORD_ROWS = 256
SCORE_ROWS = 64
MLSTM_CHUNK = 256
TOKEN_TILE = 512
MLP_TILE = 1024
FF_CHUNK = 1024
INT_MIN = -(2 ** 31)
INT_MAX = 2 ** 31 - 1
NEG_BIG = -1e30
M_INIT = -1e29
LOG2E = math.log2(math.e)
MIB = 1024 * 1024


def _rms(x, w):
    return x * lax.rsqrt(jnp.mean(x * x, axis=-1, keepdims=True) + NORM_EPS) * w


def _sigmoid(x):
    return 1.0 / (1.0 + jnp.exp(-x))


def _dot(a, b):
    return jnp.dot(a, b, preferred_element_type=F32)


def _dot_nt(a, b):
    return lax.dot_general(a, b, (((1,), (1,)), ((), ())), preferred_element_type=F32)


def _bit_transpose32(a):
    a = list(a)
    mask = 0x0000FFFF
    j = 16
    while j:
        k = 0
        while k < 32:
            t = (a[k] ^ lax.shift_right_logical(a[k | j], j)) & mask
            a[k] = a[k] ^ t
            a[k | j] = a[k | j] ^ lax.shift_left(t, j)
            k = ((k | j) + 1) & ~j
        j >>= 1
        mask = (mask ^ (mask << j)) & 0xFFFFFFFF
    return a


def _params(sem, vmem_mib):
    return pltpu.CompilerParams(dimension_semantics=sem, vmem_limit_bytes=vmem_mib * MIB)


def _ada_kernel(c_ref, w_ref, b_ref, o_ref):
    c = c_ref[...]
    act = (c * _sigmoid(c)).astype(BF16)
    o_ref[0, 0] = _dot(act, w_ref[0].astype(BF16)) + b_ref[0, 0]


def _ada(c, w_ada, b_ada):
    depth, d, _ = w_ada.shape
    bsz = c.shape[0]
    b4 = b_ada.reshape(depth, N_ADA, 1, d)
    return pl.pallas_call(
        _ada_kernel,
        grid=(depth, N_ADA),
        in_specs=[
            pl.BlockSpec((bsz, d), lambda l, k: (0, 0)),
            pl.BlockSpec((1, d, d), lambda l, k: (l, 0, k)),
            pl.BlockSpec((1, 1, 1, d), lambda l, k: (l, k, 0, 0)),
        ],
        out_specs=pl.BlockSpec((1, 1, bsz, d), lambda l, k: (l, k, 0, 0)),
        out_shape=jax.ShapeDtypeStruct((depth, N_ADA, bsz, d), F32),
        compiler_params=_params(("arbitrary", "arbitrary"), 32),
        name="ada_mod",
    )(c, w_ada, b4)


def _inproj_kernel(x_ref, mod_ref, nw_ref, w_ref, lru_ref, qkv_ref, mo_ref, aq_ref, akv_ref, misc_ref, *, dims):
    cl, cm, cq, ckv = dims
    x = x_ref[0]
    sh = mod_ref[0, 0]
    sc = mod_ref[1, 0]
    h = _rms(x, nw_ref[...]) * (1.0 + sc) + sh
    proj = _dot(h.astype(BF16), w_ref[...])
    o = 0
    lru_ref[0] = proj[:, o:o + 2 * cl]
    o += 2 * cl
    qkv_ref[0] = proj[:, o:o + 3 * cm].astype(BF16)
    o += 3 * cm
    mo_ref[0] = proj[:, o:o + cm]
    o += cm
    aq_ref[0] = proj[:, o:o + cq]
    o += cq
    akv_ref[0] = proj[:, o:o + ckv]
    o += ckv
    misc_ref[0] = proj[:, o:o + LANE]


def _inproj(x, mod, nw, w_in_p, dims):
    bsz, s, d = x.shape
    cl, cm, cq, ckv = dims
    n_out = w_in_p.shape[1]
    t = min(TOKEN_TILE, s)
    tok = lambda c: pl.BlockSpec((1, t, c), lambda b, i: (b, i, 0))
    return pl.pallas_call(
        functools.partial(_inproj_kernel, dims=dims),
        grid=(bsz, s // t),
        in_specs=[
            tok(d),
            pl.BlockSpec((N_ADA, 1, 1, d), lambda b, i: (0, b, 0, 0)),
            pl.BlockSpec((1, d), lambda b, i: (0, 0)),
            pl.BlockSpec((d, n_out), lambda b, i: (0, 0)),
        ],
        out_specs=[tok(2 * cl), tok(3 * cm), tok(cm), tok(cq), tok(ckv), tok(LANE)],
        out_shape=[
            jax.ShapeDtypeStruct((bsz, s, 2 * cl), F32),
            jax.ShapeDtypeStruct((bsz, s, 3 * cm), BF16),
            jax.ShapeDtypeStruct((bsz, s, cm), F32),
            jax.ShapeDtypeStruct((bsz, s, cq), F32),
            jax.ShapeDtypeStruct((bsz, s, ckv), F32),
            jax.ShapeDtypeStruct((bsz, s, LANE), F32),
        ],
        compiler_params=_params(("parallel", "parallel"), 56),
        name="in_proj",
    )(x, mod, nw, w_in_p)


def _lru_kernel(p_ref, cw_ref, cb_ref, wa_ref, ba_ref, wx_ref, bx_ref, lam_ref, g_ref, out_ref, hc_ref, xt_ref):
    t_len = p_ref.shape[1]
    c = p_ref.shape[2] // 2

    @pl.when(pl.program_id(1) == 0)
    def _():
        hc_ref[...] = jnp.zeros_like(hc_ref)
        xt_ref[...] = jnp.zeros_like(xt_ref)

    p = p_ref[0]
    x = p[:, :c]
    y = p[:, c:]
    cw = cw_ref[...]
    xe = jnp.concatenate([xt_ref[...], x], axis=0)
    xc = cb_ref[...] + cw[CONV_WIDTH - 1:CONV_WIDTH] * x
    for j in range(1, CONV_WIDTH):
        xc = xc + cw[CONV_WIDTH - 1 - j:CONV_WIDTH - j] * pltpu.roll(xe, j, axis=0)[8:]
    xt_ref[...] = x[t_len - 8:]

    xb = xc.astype(BF16)
    r = _sigmoid(_dot(xb, wa_ref[...]) + ba_ref[...])
    gi = _sigmoid(_dot(xb, wx_ref[...]) + bx_ref[...])
    nl = -lam_ref[...]
    softplus = jnp.maximum(nl, 0.0) + jnp.log(1.0 + jnp.exp(-jnp.abs(nl)))
    log_a = -LRU_C * r * softplus
    a = jnp.exp(log_a)
    u = jnp.sqrt(1.0 - jnp.exp(2.0 * log_a)) * (gi * xc)

    row = lax.broadcasted_iota(I32, (t_len, c), 0)
    d = 1
    while d < t_len:
        keep = row >= d
        a_s = jnp.where(keep, pltpu.roll(a, d, axis=0), 1.0)
        u_s = jnp.where(keep, pltpu.roll(u, d, axis=0), 0.0)
        u = a * u_s + u
        a = a * a_s
        d *= 2
    h = a * hc_ref[...] + u
    hc_ref[...] = h[t_len - 1:]

    gelu = 0.5 * y * (1.0 + jnp.tanh(math.sqrt(2.0 / math.pi) * (y + 0.044715 * (y * y * y))))
    out_ref[0] = _rms(gelu * h, g_ref[...]).astype(BF16)


def _lru(p_lru, conv_w, conv_b, wa_bd, ba, wx_bd, bx, lam, g_lru):
    bsz, s, c2 = p_lru.shape
    c = c2 // 2
    t = min(TOKEN_TILE, s)
    row = lambda n: pl.BlockSpec((1, n), lambda b, i: (0, 0))
    full = lambda a: pl.BlockSpec(a.shape, lambda b, i: (0, 0))
    return pl.pallas_call(
        _lru_kernel,
        grid=(bsz, s // t),
        in_specs=[
            pl.BlockSpec((1, t, c2), lambda b, i: (b, i, 0)),
            full(conv_w), row(c), full(wa_bd), row(c), full(wx_bd), row(c), row(c), row(c),
        ],
        out_specs=pl.BlockSpec((1, t, c), lambda b, i: (b, i, 0)),
        out_shape=jax.ShapeDtypeStruct((bsz, s, c), BF16),
        scratch_shapes=[pltpu.VMEM((1, c), F32), pltpu.VMEM((8, c), F32)],
        compiler_params=_params(("arbitrary", "arbitrary"), 32),
        name="rg_lru",
    )(p_lru, conv_w, conv_b, wa_bd, ba, wx_bd, bx, lam, g_lru)


def _mlstm_kernel(qkv_ref, mo_ref, misc_ref, gb_ref, nw_ref, out_ref, ct_ref, m_ref):
    l_len = qkv_ref.shape[1]
    dh = ct_ref.shape[2]
    nh = ct_ref.shape[0]

    @pl.when(pl.program_id(1) == 0)
    def _():
        ct_ref[...] = jnp.zeros_like(ct_ref)
        m_ref[...] = jnp.zeros_like(m_ref)

    g = misc_ref[0] + gb_ref[...]
    capped = GATE_SOFTCAP * jnp.tanh(g / GATE_SOFTCAP)
    logf = jnp.minimum(capped, 0.0) - jnp.log(1.0 + jnp.exp(-jnp.abs(capped)))
    row = lax.broadcasted_iota(I32, (l_len, LANE), 0)
    bcum = logf
    d = 1
    while d < l_len:
        bcum = bcum + jnp.where(row >= d, pltpu.roll(bcum, d, axis=0), 0.0)
        d *= 2
    capped_t = capped.T
    bcum_t = bcum.T
    causal = (lax.broadcasted_iota(I32, (l_len, l_len), 0) <= lax.broadcasted_iota(I32, (l_len, l_len), 1))
    ones_rows = jnp.ones((8, l_len), F32)

    for h in range(nh):
        q = qkv_ref[0, :, h * dh:(h + 1) * dh]
        k = qkv_ref[0, :, (nh + h) * dh:(nh + h + 1) * dh]
        v = qkv_ref[0, :, (2 * nh + h) * dh:(2 * nh + h + 1) * dh]
        qs = (q.astype(F32) * (dh ** -0.5)).astype(BF16)
        b_row = bcum_t[nh + h:nh + h + 1, :]
        i_row = capped_t[h:h + 1, :]
        r_col = capped[:, h:h + 1] - bcum[:, nh + h:nh + h + 1]
        m_prev = m_ref[h][:, 0:1]
        ct_old = ct_ref[h]

        dmat = jnp.where(causal, r_col + b_row, -jnp.inf)
        inter = b_row + m_prev
        m_row = jnp.maximum(inter, jnp.max(dmat, axis=0, keepdims=True))
        inter_w = jnp.exp(inter - m_row)
        smat = _dot_nt(k, qs) * jnp.exp(dmat - m_row)
        v_t = v.astype(F32).T
        cq = _dot_nt(ct_old.astype(BF16), qs)
        num = _dot(v_t.astype(BF16), smat.astype(BF16)) + inter_w * cq[:dh]
        den = jnp.sum(smat, axis=0, keepdims=True) + inter_w * cq[dh:dh + 1]
        hh = num / jnp.maximum(jnp.abs(den), jnp.exp(-m_row))

        b_last = b_row[:, l_len - 1:]
        w_row = b_last - b_row + i_row
        m_new = jnp.maximum(b_last + m_prev, jnp.max(w_row, axis=1, keepdims=True))
        decay = jnp.exp(b_last + m_prev - m_new)
        vw = jnp.concatenate([v_t, ones_rows], axis=0) * jnp.exp(w_row - m_new)
        ct_ref[h] = decay * ct_old + _dot(vw.astype(BF16), k)
        m_ref[h] = jnp.broadcast_to(m_new, (1, LANE))

        hn_t = hh * lax.rsqrt(jnp.mean(hh * hh, axis=0, keepdims=True) + NORM_EPS)
        hn = hn_t.T * nw_ref[:, h * dh:(h + 1) * dh]
        gate = _sigmoid(mo_ref[0, :, h * dh:(h + 1) * dh])
        out_ref[0, :, h * dh:(h + 1) * dh] = (gate * hn).astype(BF16)


def _mlstm(p_qkv, p_mo, p_misc, gate_bias, g_m):
    bsz, s, cm = p_mo.shape
    nh = MLSTM_HEADS
    dh = cm // nh
    l_len = min(MLSTM_CHUNK, s)
    tok = lambda c: pl.BlockSpec((1, l_len, c), lambda b, i: (b, i, 0))
    return pl.pallas_call(
        _mlstm_kernel,
        grid=(bsz, s // l_len),
        in_specs=[
            tok(3 * cm), tok(cm), tok(LANE),
            pl.BlockSpec((1, LANE), lambda b, i: (0, 0)),
            pl.BlockSpec((1, cm), lambda b, i: (0, 0)),
        ],
        out_specs=tok(cm),
        out_shape=jax.ShapeDtypeStruct((bsz, s, cm), BF16),
        scratch_shapes=[pltpu.VMEM((nh, dh + 8, dh), F32), pltpu.VMEM((nh, 1, LANE), F32)],
        compiler_params=_params(("arbitrary", "arbitrary"), 32),
        name="mlstm",
    )(p_qkv, p_mo, p_misc, gate_bias, g_m)


def _dsa_kernel(rb_ref, aq_ref, akv_ref, misck_ref, miscq_ref, qnw_ref, kvnw_ref, wq_ref, wqi_ref, wuk_ref, wuv_ref,
                g_ref, out_ref, ckv_s, ckvt_s, ik_s, key_s, bits_s, nb_s, acc_s, lg_s, *, topk):
    qi = pl.program_id(1)
    tq = aq_ref.shape[1]
    n_heads = wuk_ref.shape[0]
    dh = wuk_ref.shape[2]
    d_idx = ik_s.shape[1]

    @pl.when(qi == 0)
    def _():
        ckv = _rms(akv_ref[0], kvnw_ref[...])
        ckv_s[...] = ckv.astype(BF16)
        ckvt_s[...] = ckv.T.astype(BF16)
        ik_s[...] = misck_ref[0][:, 32:32 + d_idx].astype(BF16)
        bits_s[...] = jnp.zeros_like(bits_s)
    @pl.when((pl.program_id(0) == 0) & (qi == 0))
    def _():
        def build(rb, carry):
            r0 = pl.multiple_of(rb * BIAS_ROWS, BIAS_ROWS)
            rr = lax.broadcasted_iota(I32, (BIAS_ROWS, tq), 0) + r0
            ii = lax.broadcasted_iota(I32, (BIAS_ROWS, tq), 1)
            n = jnp.maximum(ii + 3 * tq - rr, 0)
            log_ratio = jnp.log(jnp.maximum(n, 1).astype(F32) / REL_MAX_EXACT) / math.log(REL_MAX_DIST / REL_MAX_EXACT)
            large = jnp.minimum(REL_MAX_EXACT + (log_ratio * (REL_BUCKETS - REL_MAX_EXACT)).astype(I32),
                                REL_BUCKETS - 1)
            bucket = jnp.where(n < REL_MAX_EXACT, n, large)
            for h in range(n_heads):
                last = rb_ref[REL_BUCKETS - 1, h]
                val = jnp.zeros((BIAS_ROWS, tq), F32)
                for kb in range(REL_BUCKETS - 1):
                    val = jnp.where(bucket == kb, (rb_ref[kb, h] - last) * LOG2E, val)
                nb_s[h, pl.ds(r0, BIAS_ROWS), :] = val
            return carry

        lax.fori_loop(0, nb_s.shape[1] // BIAS_ROWS, build, 0)


    q_lat_t = _rms(aq_ref[0], qnw_ref[...]).T.astype(BF16)
    q_t = _dot(wq_ref[...], q_lat_t).astype(BF16)
    qa_all_t = jnp.concatenate(
        [(_dot(wuk_ref[h], q_t[h * dh:(h + 1) * dh]) * (dh ** -0.5 * LOG2E)).astype(BF16) for h in range(n_heads)],
        axis=1)
    q_idx_t = _dot(wqi_ref[...], q_lat_t) * (d_idx ** -0.5)
    q_idx_all = jnp.concatenate([q_idx_t[h * d_idx:(h + 1) * d_idx] for h in range(IDX_HEADS)], axis=1).astype(BF16)
    w_i = miscq_ref[0].T[8:8 + IDX_HEADS] * (IDX_HEADS ** -0.5)

    n_chunks = lax.div(qi + KV_CHUNK // tq, KV_CHUNK // tq)
    t_pos = qi * tq + lax.broadcasted_iota(I32, (SCORE_ROWS, tq), 1)
    row = lax.broadcasted_iota(I32, (SCORE_ROWS, tq), 0)

    def score_group(g):
        for sub in range(WORD_ROWS // SCORE_ROWS):
            r0 = pl.multiple_of(g * WORD_ROWS + sub * SCORE_ROWS, SCORE_ROWS)
            d_all = _dot(ik_s[pl.ds(r0, SCORE_ROWS), :], q_idx_all)
            sc = jnp.zeros((SCORE_ROWS, tq), F32)
            for h in range(IDX_HEADS):
                sc = sc + jnp.maximum(d_all[:, h * tq:(h + 1) * tq], 0.0) * w_i[h:h + 1]
            bits = lax.bitcast_convert_type(sc, I32)
            key = jnp.where(bits < 0, bits ^ INT_MAX, bits)
            key_s[pl.ds(r0, SCORE_ROWS), :] = jnp.where(row + r0 <= t_pos, key, INT_MIN)

    lane_tiles = tq // LANE

    def slice_group(g):
        k0 = pl.multiple_of(g * WORD_ROWS, WORD_ROWS)
        w0 = pl.multiple_of(g * 8, 8)
        for lt in range(lane_tiles):
            cols = slice(lt * LANE, (lt + 1) * LANE)
            w = _bit_transpose32([key_s[pl.ds(k0 + j * 8, 8), cols] ^ INT_MIN for j in range(32)])
            for i in range(32):
                bits_s[31 - i, pl.ds(w0, 8), cols] = w[i]

    def score_and_slice(g, carry):
        slice_group(g - 1)
        score_group(g)
        return carry

    n_word_groups = n_chunks * (KV_CHUNK // WORD_ROWS)
    score_group(0)
    lax.fori_loop(1, n_word_groups, score_and_slice, 0)
    slice_group(n_word_groups - 1)

    n_word_rows = bits_s.shape[1]
    wrow = lax.broadcasted_iota(I32, (n_word_rows, tq), 0)
    first_key = lax.shift_right_logical(wrow, 3) * WORD_ROWS + (wrow & 7)
    t_lane = qi * tq + lax.broadcasted_iota(I32, (n_word_rows, tq), 1)
    n_valid = jnp.clip(lax.shift_right_arithmetic(t_lane - first_key, 3) + 1, 0, 32)
    eq0 = jnp.where(n_valid <= 0, 0, lax.shift_left(jnp.int32(-1), 32 - jnp.maximum(n_valid, 1)))

    def bit_step(i, carry):
        eq, c_gt, ans = carry
        b = 31 - i
        t = eq & bits_s[b]
        cnt = jnp.sum(lax.population_count(t), axis=0, keepdims=True)
        take = (c_gt + cnt) >= topk
        eq = jnp.where(take, t, eq ^ t)
        c_gt = jnp.where(take, c_gt, c_gt + cnt)
        ans = jnp.where(take, ans | lax.shift_left(jnp.int32(1), b), ans)
        return eq, c_gt, ans

    zero_row = jnp.zeros((1, tq), I32)
    _, _, ans = lax.fori_loop(0, 32, bit_step, (eq0, zero_row, zero_row))
    thr = jnp.maximum(ans ^ INT_MIN, INT_MIN + 1)

    c_last = lax.shift_right_logical(qi, 1)
    acc_s[...] = jnp.zeros_like(acc_s)

    def masked_logits(c):
        k0 = pl.multiple_of(c * KV_CHUNK, KV_CHUNK)
        sel = key_s[pl.ds(k0, KV_CHUNK), :] >= thr
        logits_all = _dot(ckv_s[pl.ds(k0, KV_CHUNK), :], qa_all_t)
        b0 = pl.multiple_of(jnp.maximum(2 * c - qi + 3, 0) * tq, tq)
        slot = c & 1
        mx = []
        for h in range(n_heads):
            logits = logits_all[:, h * tq:(h + 1) * tq] + nb_s[h, pl.ds(b0, KV_CHUNK), :]
            logits = jnp.where(sel, logits, NEG_BIG)
            lg_s[slot, :, h * tq:(h + 1) * tq] = logits
            mx.append(jnp.max(logits, axis=0, keepdims=True))
        return tuple(mx)

    def softmax_update(c, mx, m_old, l_old):
        k0 = pl.multiple_of(c * KV_CHUNK, KV_CHUNK)
        slot = c & 1
        m_out, l_out, alphas, ps = [], [], [], []
        for h in range(n_heads):
            m_new = jnp.maximum(m_old[h], mx[h])
            alpha = jnp.exp2(m_old[h] - m_new)
            p = jnp.exp2(lg_s[slot, :, h * tq:(h + 1) * tq] - m_new)
            l_out.append(alpha * l_old[h] + jnp.sum(p, axis=0, keepdims=True))
            m_out.append(m_new)
            alphas.append(alpha)
            ps.append(p.astype(BF16))
        acc_s[...] = (jnp.concatenate(alphas, axis=1) * acc_s[...]
                      + _dot(ckvt_s[:, pl.ds(k0, KV_CHUNK)], jnp.concatenate(ps, axis=1)))
        return tuple(m_out), tuple(l_out)

    def attend(c, carry):
        mx, m_old, l_old = carry
        m_new, l_new = softmax_update(c, mx, m_old, l_old)
        return masked_logits(c + 1), m_new, l_new

    carry = (masked_logits(0),
             tuple(jnp.full((1, tq), M_INIT, F32) for _ in range(n_heads)),
             tuple(jnp.zeros((1, tq), F32) for _ in range(n_heads)))
    carry = lax.fori_loop(0, c_last, attend, carry)
    _, l_fin = softmax_update(c_last, *carry)
    outs = [_dot(wuv_ref[h], (acc_s[:, h * tq:(h + 1) * tq] / l_fin[h]).astype(BF16)) for h in range(n_heads)]
    o = jnp.concatenate(outs, axis=0).T
    out_ref[0] = _rms(o, g_ref[...]).astype(BF16)


def _dsa(rel_bias, p_aq, p_akv, p_misc, qnw, kvnw, wq, wqi, wuk_t, wuv_h, g_a):
    bsz, s, cq = p_aq.shape
    ckv = p_akv.shape[2]
    d_idx = wqi.shape[1] // IDX_HEADS
    topk = min(INDEX_TOPK, s // 4)
    tq = Q_BLOCK
    assert s % KV_CHUNK == 0 and s % tq == 0
    cw = wq.shape[1]
    full2 = lambda a: pl.BlockSpec(a.shape, lambda b, i: (0, 0))
    full3 = lambda a: pl.BlockSpec(a.shape, lambda b, i: (0, 0, 0))
    return pl.pallas_call(
        functools.partial(_dsa_kernel, topk=topk),
        grid=(bsz, s // tq),
        in_specs=[
            pl.BlockSpec(memory_space=pltpu.SMEM),
            pl.BlockSpec((1, tq, cq), lambda b, i: (b, i, 0)),
            pl.BlockSpec((1, s, ckv), lambda b, i: (b, 0, 0)),
            pl.BlockSpec((1, s, LANE), lambda b, i: (b, 0, 0)),
            pl.BlockSpec((1, tq, LANE), lambda b, i: (b, i, 0)),
            full2(qnw), full2(kvnw), full2(wq), full2(wqi), full3(wuk_t), full3(wuv_h), full2(g_a),
        ],
        out_specs=pl.BlockSpec((1, tq, cw), lambda b, i: (b, i, 0)),
        out_shape=jax.ShapeDtypeStruct((bsz, s, cw), BF16),
        scratch_shapes=[
            pltpu.VMEM((s, ckv), BF16),
            pltpu.VMEM((ckv, s), BF16),
            pltpu.VMEM((s, d_idx), BF16),
            pltpu.VMEM((s, tq), I32),
            pltpu.VMEM((32, s // 32, tq), I32),
            pltpu.VMEM((ATTN_HEADS, 5 * tq, tq), F32),
            pltpu.VMEM((ckv, ATTN_HEADS * tq), F32),
            pltpu.VMEM((2, KV_CHUNK, ATTN_HEADS * tq), F32),
        ],
        compiler_params=_params(("arbitrary", "arbitrary"), 48),
        name="dsa",
    )(rel_bias, p_aq, p_akv, p_misc, p_misc, qnw, kvnw, wq, wqi, wuk_t, wuv_h, g_a)


def _out_mlp_kernel(x_ref, yl_ref, ym_ref, ya_ref, mod_ref, wo_ref, n2_ref, w1_ref, w2_ref, fw_ref, out_ref,
                    h2_s, acc_s, *, final):
    j = pl.program_id(2)

    @pl.when(j == 0)
    def _():
        cat = jnp.concatenate([yl_ref[0], ym_ref[0], ya_ref[0]], axis=1)
        x1 = x_ref[0] + mod_ref[2, 0] * _dot(cat, wo_ref[...])
        out_ref[0] = x1
        h2_s[...] = (_rms(x1, n2_ref[...]) * (1.0 + mod_ref[4, 0]) + mod_ref[3, 0]).astype(BF16)
        acc_s[...] = jnp.zeros_like(acc_s)

    a = jnp.maximum(_dot(h2_s[...], w1_ref[...]), 0.0)
    acc_s[...] += _dot((a * a).astype(BF16), w2_ref[...])

    @pl.when(j == pl.num_programs(2) - 1)
    def _():
        x2 = out_ref[0] + mod_ref[5, 0] * acc_s[...]
        if final:
            x2 = _rms(x2, fw_ref[...])
        out_ref[0] = x2


def _out_mlp(x, y_lru, y_m, y_a, mod, wo, n2w, w1, w2, fw, final):
    bsz, s, d = x.shape
    dff = w1.shape[1]
    t = min(MLP_TILE, s)
    fc = min(FF_CHUNK, dff)
    tok = lambda c: pl.BlockSpec((1, t, c), lambda b, i, j: (b, i, 0))
    return pl.pallas_call(
        functools.partial(_out_mlp_kernel, final=final),
        grid=(bsz, s // t, dff // fc),
        in_specs=[
            tok(d), tok(y_lru.shape[2]), tok(y_m.shape[2]), tok(y_a.shape[2]),
            pl.BlockSpec((N_ADA, 1, 1, d), lambda b, i, j: (0, b, 0, 0)),
            pl.BlockSpec((d, d), lambda b, i, j: (0, 0)),
            pl.BlockSpec((1, d), lambda b, i, j: (0, 0)),
            pl.BlockSpec((d, fc), lambda b, i, j: (0, j)),
            pl.BlockSpec((fc, d), lambda b, i, j: (j, 0)),
            pl.BlockSpec((1, d), lambda b, i, j: (0, 0)),
        ],
        out_specs=tok(d),
        out_shape=jax.ShapeDtypeStruct((bsz, s, d), F32),
        scratch_shapes=[pltpu.VMEM((t, d), BF16), pltpu.VMEM((t, d), F32)],
        compiler_params=_params(("parallel", "parallel", "arbitrary"), 56),
        name="out_mlp",
    )(x, y_lru, y_m, y_a, mod, wo, n2w, w1, w2, fw)


def _block_diag(w):
    nb, bi, bo = w.shape
    out = jnp.zeros((nb * bi, nb * bo), w.dtype)
    for n in range(nb):
        out = out.at[n * bi:(n + 1) * bi, n * bo:(n + 1) * bo].set(w[n])
    return out


def kernel(x, c, w_in, conv_w, conv_b, lru_wa, lru_ba, lru_wx, lru_bx, lru_lambda, mlstm_bi, mlstm_bf, w_q_up,
           w_qidx_up, w_uk, w_uv, q_lat_norm_w, kv_lat_norm_w, rel_bias, group_norm_w, w_o, w_ada, b_ada, norm1_w,
           norm2_w, w_mlp1, w_mlp2, final_norm_w):
    depth, d, _ = w_in.shape
    bsz = x.shape[0]
    cl = conv_w.shape[2]
    cm = d // 2
    cq = w_q_up.shape[1]
    ckv = w_uk.shape[1]
    d_idx = w_qidx_up.shape[3]
    nh = MLSTM_HEADS
    dims = (cl, cm, cq, ckv)

    mod_all = _ada(c, w_ada, b_ada).reshape(depth, N_ADA, bsz, 1, d)

    o_gate = 2 * cl + 4 * cm
    o_aq = o_gate + 2 * nh
    o_akv = o_aq + cq
    o_ik = o_akv + ckv
    o_iw = o_ik + d_idx
    zeros = lambda n: jnp.zeros((depth, d, n), w_in.dtype)
    w_in_p = jnp.concatenate([
        w_in[:, :, :o_gate], w_in[:, :, o_aq:o_akv], w_in[:, :, o_akv:o_ik],
        w_in[:, :, o_gate:o_aq], w_in[:, :, o_iw:o_iw + IDX_HEADS], zeros(32 - 2 * nh - IDX_HEADS),
        w_in[:, :, o_ik:o_iw], zeros(LANE - 32 - d_idx),
    ], axis=2).astype(BF16)

    gate_bias = jnp.concatenate([mlstm_bi, mlstm_bf, jnp.zeros((depth, LANE - 2 * nh), F32)], axis=1)
    fw = final_norm_w.reshape(1, d)

    for l in range(depth):
        mod = mod_all[l]
        p_lru, p_qkv, p_mo, p_aq, p_akv, p_misc = _inproj(x, mod, norm1_w[l].reshape(1, d), w_in_p[l], dims)
        gw = group_norm_w[l]
        y_lru = _lru(p_lru, conv_w[l], conv_b[l].reshape(1, cl),
                     _block_diag(lru_wa[l]).astype(BF16), lru_ba[l].reshape(1, cl),
                     _block_diag(lru_wx[l]).astype(BF16), lru_bx[l].reshape(1, cl),
                     lru_lambda[l].reshape(1, cl), gw[:cl].reshape(1, cl))
        y_m = _mlstm(p_qkv, p_mo, p_misc, gate_bias[l].reshape(1, LANE), gw[cl:cl + cm].reshape(1, cm))
        wq = w_q_up[l].reshape(cq, -1).T.astype(BF16)
        wqi = w_qidx_up[l].reshape(cq, -1).T.astype(BF16)
        wuk_t = jnp.transpose(w_uk[l], (1, 0, 2)).astype(BF16)
        wuv_h = jnp.transpose(w_uv[l], (1, 2, 0)).astype(BF16)
        y_a = _dsa(rel_bias, p_aq, p_akv, p_misc, q_lat_norm_w[l].reshape(1, cq), kv_lat_norm_w[l].reshape(1, ckv),
                   wq, wqi, wuk_t, wuv_h, gw[cl + cm:].reshape(1, -1))
        x = _out_mlp(x, y_lru, y_m, y_a, mod, w_o[l].astype(BF16), norm2_w[l].reshape(1, d),
                     w_mlp1[l].astype(BF16), w_mlp2[l].astype(BF16), fw, final=(l == depth - 1))
    return x
```

```python
import functools
import math

import jax
import jax.numpy as jnp
from jax import lax
from jax.experimental import pallas as pl
from jax.experimental.pallas import tpu as pltpu

F32 = jnp.float32
BF16 = jnp.bfloat16
I32 = jnp.int32

NORM_EPS = 1e-6
LRU_BLOCKS = 4
CONV_WIDTH = 4
LRU_C = 8.0
MLSTM_HEADS = 4
GATE_SOFTCAP = 15.0
ATTN_HEADS = 4
IDX_HEADS = 8
INDEX_TOPK = 256
REL_BUCKETS = 32
REL_MAX_EXACT = 16
REL_MAX_DIST = 128
N_ADA = 6

LANE = 128
Q_BLOCK = 256
KV_CHUNK = 2 * Q_BLOCK
BIAS_ROWS = 64
WORD_ROWS = 256
SCORE_ROWS = 128
MLSTM_CHUNK = 256
TOKEN_TILE = 512
MLP_TILE = 1024
FF_CHUNK = 1024
INT_MIN = -(2 ** 31)
INT_MAX = 2 ** 31 - 1
NEG_BIG = -1e30
M_INIT = -1e29
LOG2E = math.log2(math.e)
MIB = 1024 * 1024


def _rms(x, w):
    return x * lax.rsqrt(jnp.mean(x * x, axis=-1, keepdims=True) + NORM_EPS) * w


def _sigmoid(x):
    return 1.0 / (1.0 + jnp.exp(-x))


def _dot(a, b):
    return jnp.dot(a, b, preferred_element_type=F32)


def _dot_nt(a, b):
    return lax.dot_general(a, b, (((1,), (1,)), ((), ())), preferred_element_type=F32)


def _bit_transpose32(a):
    a = list(a)
    mask = 0x0000FFFF
    j = 16
    while j:
        k = 0
        while k < 32:
            t = (a[k] ^ lax.shift_right_logical(a[k | j], j)) & mask
            a[k] = a[k] ^ t
            a[k | j] = a[k | j] ^ lax.shift_left(t, j)
            k = ((k | j) + 1) & ~j
        j >>= 1
        mask = (mask ^ (mask << j)) & 0xFFFFFFFF
    return a


def _params(sem, vmem_mib):
    return pltpu.CompilerParams(dimension_semantics=sem, vmem_limit_bytes=vmem_mib * MIB)


def _ada_kernel(c_ref, w_ref, b_ref, o_ref):
    c = c_ref[...]
    act = (c * _sigmoid(c)).astype(BF16)
    o_ref[0, 0] = _dot(act, w_ref[0].astype(BF16)) + b_ref[0, 0]


def _ada(c, w_ada, b_ada):
    depth, d, _ = w_ada.shape
    bsz = c.shape[0]
    b4 = b_ada.reshape(depth, N_ADA, 1, d)
    return pl.pallas_call(
        _ada_kernel,
        grid=(depth, N_ADA),
        in_specs=[
            pl.BlockSpec((bsz, d), lambda l, k: (0, 0)),
            pl.BlockSpec((1, d, d), lambda l, k: (l, 0, k)),
            pl.BlockSpec((1, 1, 1, d), lambda l, k: (l, k, 0, 0)),
        ],
        out_specs=pl.BlockSpec((1, 1, bsz, d), lambda l, k: (l, k, 0, 0)),
        out_shape=jax.ShapeDtypeStruct((depth, N_ADA, bsz, d), F32),
        compiler_params=_params(("arbitrary", "arbitrary"), 32),
        name="ada_mod",
    )(c, w_ada, b4)


def _inproj_kernel(x_ref, mod_ref, nw_ref, w_ref, lru_ref, qkv_ref, mo_ref, aq_ref, akv_ref, misc_ref, *, dims):
    cl, cm, cq, ckv = dims
    x = x_ref[0]
    sh = mod_ref[0, 0]
    sc = mod_ref[1, 0]
    h = _rms(x, nw_ref[...]) * (1.0 + sc) + sh
    proj = _dot(h.astype(BF16), w_ref[...])
    o = 0
    lru_ref[0] = proj[:, o:o + 2 * cl]
    o += 2 * cl
    qkv_ref[0] = proj[:, o:o + 3 * cm].astype(BF16)
    o += 3 * cm
    mo_ref[0] = proj[:, o:o + cm]
    o += cm
    aq_ref[0] = proj[:, o:o + cq]
    o += cq
    akv_ref[0] = proj[:, o:o + ckv]
    o += ckv
    misc_ref[0] = proj[:, o:o + LANE]


def _inproj(x, mod, nw, w_in_p, dims):
    bsz, s, d = x.shape
    cl, cm, cq, ckv = dims
    n_out = w_in_p.shape[1]
    t = min(TOKEN_TILE, s)
    tok = lambda c: pl.BlockSpec((1, t, c), lambda b, i: (b, i, 0))
    return pl.pallas_call(
        functools.partial(_inproj_kernel, dims=dims),
        grid=(bsz, s // t),
        in_specs=[
            tok(d),
            pl.BlockSpec((N_ADA, 1, 1, d), lambda b, i: (0, b, 0, 0)),
            pl.BlockSpec((1, d), lambda b, i: (0, 0)),
            pl.BlockSpec((d, n_out), lambda b, i: (0, 0)),
        ],
        out_specs=[tok(2 * cl), tok(3 * cm), tok(cm), tok(cq), tok(ckv), tok(LANE)],
        out_shape=[
            jax.ShapeDtypeStruct((bsz, s, 2 * cl), F32),
            jax.ShapeDtypeStruct((bsz, s, 3 * cm), BF16),
            jax.ShapeDtypeStruct((bsz, s, cm), F32),
            jax.ShapeDtypeStruct((bsz, s, cq), F32),
            jax.ShapeDtypeStruct((bsz, s, ckv), F32),
            jax.ShapeDtypeStruct((bsz, s, LANE), F32),
        ],
        compiler_params=_params(("parallel", "parallel"), 56),
        name="in_proj",
    )(x, mod, nw, w_in_p)


def _lru_kernel(p_ref, cw_ref, cb_ref, wa_ref, ba_ref, wx_ref, bx_ref, lam_ref, g_ref, out_ref, hc_ref, xt_ref):
    t_len = p_ref.shape[1]
    c = p_ref.shape[2] // 2

    @pl.when(pl.program_id(1) == 0)
    def _():
        hc_ref[...] = jnp.zeros_like(hc_ref)
        xt_ref[...] = jnp.zeros_like(xt_ref)

    p = p_ref[0]
    x = p[:, :c]
    y = p[:, c:]
    cw = cw_ref[...]
    xe = jnp.concatenate([xt_ref[...], x], axis=0)
    xc = cb_ref[...] + cw[CONV_WIDTH - 1:CONV_WIDTH] * x
    for j in range(1, CONV_WIDTH):
        xc = xc + cw[CONV_WIDTH - 1 - j:CONV_WIDTH - j] * pltpu.roll(xe, j, axis=0)[8:]
    xt_ref[...] = x[t_len - 8:]

    xb = xc.astype(BF16)
    r = _sigmoid(_dot(xb, wa_ref[...]) + ba_ref[...])
    gi = _sigmoid(_dot(xb, wx_ref[...]) + bx_ref[...])
    nl = -lam_ref[...]
    softplus = jnp.maximum(nl, 0.0) + jnp.log(1.0 + jnp.exp(-jnp.abs(nl)))
    log_a = -LRU_C * r * softplus
    a = jnp.exp(log_a)
    u = jnp.sqrt(1.0 - jnp.exp(2.0 * log_a)) * (gi * xc)

    row = lax.broadcasted_iota(I32, (t_len, c), 0)
    d = 1
    while d < t_len:
        keep = row >= d
        a_s = jnp.where(keep, pltpu.roll(a, d, axis=0), 1.0)
        u_s = jnp.where(keep, pltpu.roll(u, d, axis=0), 0.0)
        u = a * u_s + u
        a = a * a_s
        d *= 2
    h = a * hc_ref[...] + u
    hc_ref[...] = h[t_len - 1:]

    gelu = 0.5 * y * (1.0 + jnp.tanh(math.sqrt(2.0 / math.pi) * (y + 0.044715 * (y * y * y))))
    out_ref[0] = _rms(gelu * h, g_ref[...]).astype(BF16)


def _lru(p_lru, conv_w, conv_b, wa_bd, ba, wx_bd, bx, lam, g_lru):
    bsz, s, c2 = p_lru.shape
    c = c2 // 2
    t = min(TOKEN_TILE, s)
    row = lambda n: pl.BlockSpec((1, n), lambda b, i: (0, 0))
    full = lambda a: pl.BlockSpec(a.shape, lambda b, i: (0, 0))
    return pl.pallas_call(
        _lru_kernel,
        grid=(bsz, s // t),
        in_specs=[
            pl.BlockSpec((1, t, c2), lambda b, i: (b, i, 0)),
            full(conv_w), row(c), full(wa_bd), row(c), full(wx_bd), row(c), row(c), row(c),
        ],
        out_specs=pl.BlockSpec((1, t, c), lambda b, i: (b, i, 0)),
        out_shape=jax.ShapeDtypeStruct((bsz, s, c), BF16),
        scratch_shapes=[pltpu.VMEM((1, c), F32), pltpu.VMEM((8, c), F32)],
        compiler_params=_params(("arbitrary", "arbitrary"), 32),
        name="rg_lru",
    )(p_lru, conv_w, conv_b, wa_bd, ba, wx_bd, bx, lam, g_lru)


def _mlstm_kernel(qkv_ref, mo_ref, misc_ref, gb_ref, nw_ref, out_ref, ct_ref, m_ref):
    l_len = qkv_ref.shape[1]
    dh = ct_ref.shape[2]
    nh = ct_ref.shape[0]

    @pl.when(pl.program_id(1) == 0)
    def _():
        ct_ref[...] = jnp.zeros_like(ct_ref)
        m_ref[...] = jnp.zeros_like(m_ref)

    g = misc_ref[0] + gb_ref[...]
    capped = GATE_SOFTCAP * jnp.tanh(g / GATE_SOFTCAP)
    logf = jnp.minimum(capped, 0.0) - jnp.log(1.0 + jnp.exp(-jnp.abs(capped)))
    row = lax.broadcasted_iota(I32, (l_len, LANE), 0)
    bcum = logf
    d = 1
    while d < l_len:
        bcum = bcum + jnp.where(row >= d, pltpu.roll(bcum, d, axis=0), 0.0)
        d *= 2
    capped_t = capped.T
    bcum_t = bcum.T
    causal = (lax.broadcasted_iota(I32, (l_len, l_len), 0) <= lax.broadcasted_iota(I32, (l_len, l_len), 1))
    ones_rows = jnp.ones((8, l_len), F32)

    for h in range(nh):
        q = qkv_ref[0, :, h * dh:(h + 1) * dh]
        k = qkv_ref[0, :, (nh + h) * dh:(nh + h + 1) * dh]
        v = qkv_ref[0, :, (2 * nh + h) * dh:(2 * nh + h + 1) * dh]
        qs = (q.astype(F32) * (dh ** -0.5)).astype(BF16)
        b_row = bcum_t[nh + h:nh + h + 1, :]
        i_row = capped_t[h:h + 1, :]
        r_col = capped[:, h:h + 1] - bcum[:, nh + h:nh + h + 1]
        m_prev = m_ref[h][:, 0:1]
        ct_old = ct_ref[h]

        dmat = jnp.where(causal, r_col + b_row, -jnp.inf)
        inter = b_row + m_prev
        m_row = jnp.maximum(inter, jnp.max(dmat, axis=0, keepdims=True))
        inter_w = jnp.exp(inter - m_row)
        smat = _dot_nt(k, qs) * jnp.exp(dmat - m_row)
        v_t = v.astype(F32).T
        cq = _dot_nt(ct_old.astype(BF16), qs)
        num = _dot(v_t.astype(BF16), smat.astype(BF16)) + inter_w * cq[:dh]
        den = jnp.sum(smat, axis=0, keepdims=True) + inter_w * cq[dh:dh + 1]
        hh = num / jnp.maximum(jnp.abs(den), jnp.exp(-m_row))

        b_last = b_row[:, l_len - 1:]
        w_row = b_last - b_row + i_row
        m_new = jnp.maximum(b_last + m_prev, jnp.max(w_row, axis=1, keepdims=True))
        decay = jnp.exp(b_last + m_prev - m_new)
        vw = jnp.concatenate([v_t, ones_rows], axis=0) * jnp.exp(w_row - m_new)
        ct_ref[h] = decay * ct_old + _dot(vw.astype(BF16), k)
        m_ref[h] = jnp.broadcast_to(m_new, (1, LANE))

        hn_t = hh * lax.rsqrt(jnp.mean(hh * hh, axis=0, keepdims=True) + NORM_EPS)
        hn = hn_t.T * nw_ref[:, h * dh:(h + 1) * dh]
        gate = _sigmoid(mo_ref[0, :, h * dh:(h + 1) * dh])
        out_ref[0, :, h * dh:(h + 1) * dh] = (gate * hn).astype(BF16)


def _mlstm(p_qkv, p_mo, p_misc, gate_bias, g_m):
    bsz, s, cm = p_mo.shape
    nh = MLSTM_HEADS
    dh = cm // nh
    l_len = min(MLSTM_CHUNK, s)
    tok = lambda c: pl.BlockSpec((1, l_len, c), lambda b, i: (b, i, 0))
    return pl.pallas_call(
        _mlstm_kernel,
        grid=(bsz, s // l_len),
        in_specs=[
            tok(3 * cm), tok(cm), tok(LANE),
            pl.BlockSpec((1, LANE), lambda b, i: (0, 0)),
            pl.BlockSpec((1, cm), lambda b, i: (0, 0)),
        ],
        out_specs=tok(cm),
        out_shape=jax.ShapeDtypeStruct((bsz, s, cm), BF16),
        scratch_shapes=[pltpu.VMEM((nh, dh + 8, dh), F32), pltpu.VMEM((nh, 1, LANE), F32)],
        compiler_params=_params(("arbitrary", "arbitrary"), 32),
        name="mlstm",
    )(p_qkv, p_mo, p_misc, gate_bias, g_m)


def _dsa_kernel(rb_ref, aq_ref, akv_ref, misck_ref, miscq_ref, qnw_ref, kvnw_ref, wq_ref, wqi_ref, wuk_ref, wuv_ref,
                g_ref, out_ref, ckv_s, ckvt_s, ik_s, key_s, bits_s, nb_s, acc_s, lg_s, *, topk):
    qi = pl.program_id(1)
    tq = aq_ref.shape[1]
    n_heads = wuk_ref.shape[0]
    dh = wuk_ref.shape[2]
    d_idx = ik_s.shape[1]

    @pl.when(qi == 0)
    def _():
        ckv = _rms(akv_ref[0], kvnw_ref[...])
        ckv_s[...] = ckv.astype(BF16)
        ckvt_s[...] = ckv.T.astype(BF16)
        ik_s[...] = misck_ref[0][:, 32:32 + d_idx].astype(BF16)
        bits_s[...] = jnp.zeros_like(bits_s)
    @pl.when((pl.program_id(0) == 0) & (qi == 0))
    def _():
        def build(rb, carry):
            r0 = pl.multiple_of(rb * BIAS_ROWS, BIAS_ROWS)
            rr = lax.broadcasted_iota(I32, (BIAS_ROWS, tq), 0) + r0
            ii = lax.broadcasted_iota(I32, (BIAS_ROWS, tq), 1)
            n = jnp.maximum(ii + 3 * tq - rr, 0)
            log_ratio = jnp.log(jnp.maximum(n, 1).astype(F32) / REL_MAX_EXACT) / math.log(REL_MAX_DIST / REL_MAX_EXACT)
            large = jnp.minimum(REL_MAX_EXACT + (log_ratio * (REL_BUCKETS - REL_MAX_EXACT)).astype(I32),
                                REL_BUCKETS - 1)
            bucket = jnp.where(n < REL_MAX_EXACT, n, large)
            for h in range(n_heads):
                last = rb_ref[REL_BUCKETS - 1, h]
                val = jnp.zeros((BIAS_ROWS, tq), F32)
                for kb in range(REL_BUCKETS - 1):
                    val = jnp.where(bucket == kb, (rb_ref[kb, h] - last) * LOG2E, val)
                nb_s[h, pl.ds(r0, BIAS_ROWS), :] = val
            return carry

        lax.fori_loop(0, nb_s.shape[1] // BIAS_ROWS, build, 0)


    q_lat_t = _rms(aq_ref[0], qnw_ref[...]).T.astype(BF16)
    q_t = _dot(wq_ref[...], q_lat_t).astype(BF16)
    qa_all_t = jnp.concatenate(
        [(_dot(wuk_ref[h], q_t[h * dh:(h + 1) * dh]) * (dh ** -0.5 * LOG2E)).astype(BF16) for h in range(n_heads)],
        axis=1)
    q_idx_t = _dot(wqi_ref[...], q_lat_t) * (d_idx ** -0.5)
    q_idx_all = jnp.concatenate([q_idx_t[h * d_idx:(h + 1) * d_idx] for h in range(IDX_HEADS)], axis=1).astype(BF16)
    w_i = miscq_ref[0].T[8:8 + IDX_HEADS] * (IDX_HEADS ** -0.5)

    n_chunks = lax.div(qi + KV_CHUNK // tq, KV_CHUNK // tq)
    t_pos = qi * tq + lax.broadcasted_iota(I32, (SCORE_ROWS, tq), 1)
    row = lax.broadcasted_iota(I32, (SCORE_ROWS, tq), 0)

    def score_group(g):
        for sub in range(WORD_ROWS // SCORE_ROWS):
            r0 = pl.multiple_of(g * WORD_ROWS + sub * SCORE_ROWS, SCORE_ROWS)
            d_all = _dot(ik_s[pl.ds(r0, SCORE_ROWS), :], q_idx_all)
            sc = jnp.zeros((SCORE_ROWS, tq), F32)
            for h in range(IDX_HEADS):
                sc = sc + jnp.maximum(d_all[:, h * tq:(h + 1) * tq], 0.0) * w_i[h:h + 1]
            bits = lax.bitcast_convert_type(sc, I32)
            key = jnp.where(bits < 0, bits ^ INT_MAX, bits)
            key_s[pl.ds(r0, SCORE_ROWS), :] = jnp.where(row + r0 <= t_pos, key, INT_MIN)

    lane_tiles = tq // LANE

    def slice_group(g):
        k0 = pl.multiple_of(g * WORD_ROWS, WORD_ROWS)
        w0 = pl.multiple_of(g * 8, 8)
        for lt in range(lane_tiles):
            cols = slice(lt * LANE, (lt + 1) * LANE)
            w = _bit_transpose32([key_s[pl.ds(k0 + j * 8, 8), cols] ^ INT_MIN for j in range(32)])
            for i in range(32):
                bits_s[31 - i, pl.ds(w0, 8), cols] = w[i]

    def score_and_slice(g, carry):
        slice_group(g - 1)
        score_group(g)
        return carry

    n_word_groups = (qi + 1) * (tq // WORD_ROWS)

    @pl.when((qi & 1) == 0)
    def _():
        key_s[pl.ds(pl.multiple_of((qi + 1) * tq, tq), tq), :] = jnp.full((tq, tq), INT_MIN, I32)

    score_group(0)
    lax.fori_loop(1, n_word_groups, score_and_slice, 0)
    slice_group(n_word_groups - 1)

    n_word_rows = bits_s.shape[1]
    wrow = lax.broadcasted_iota(I32, (n_word_rows, tq), 0)
    first_key = lax.shift_right_logical(wrow, 3) * WORD_ROWS + (wrow & 7)
    t_lane = qi * tq + lax.broadcasted_iota(I32, (n_word_rows, tq), 1)
    n_valid = jnp.clip(lax.shift_right_arithmetic(t_lane - first_key, 3) + 1, 0, 32)
    eq0 = jnp.where(n_valid <= 0, 0, lax.shift_left(jnp.int32(-1), 32 - jnp.maximum(n_valid, 1)))

    def bit_step(i, carry):
        eq, c_gt, ans = carry
        b = 31 - i
        t = eq & bits_s[b]
        cnt = jnp.sum(lax.population_count(t), axis=0, keepdims=True)
        take = (c_gt + cnt) >= topk
        eq = jnp.where(take, t, eq ^ t)
        c_gt = jnp.where(take, c_gt, c_gt + cnt)
        ans = jnp.where(take, ans | lax.shift_left(jnp.int32(1), b), ans)
        return eq, c_gt, ans

    zero_row = jnp.zeros((1, tq), I32)
    _, _, ans = lax.fori_loop(0, 32, bit_step, (eq0, zero_row, zero_row))
    thr = jnp.maximum(ans ^ INT_MIN, INT_MIN + 1)

    c_last = lax.shift_right_logical(qi, 1)
    acc_s[...] = jnp.zeros_like(acc_s)

    def masked_logits(c, slot):
        k0 = pl.multiple_of(c * KV_CHUNK, KV_CHUNK)
        sel = key_s[pl.ds(k0, KV_CHUNK), :] >= thr
        logits_all = _dot(ckv_s[pl.ds(k0, KV_CHUNK), :], qa_all_t)
        b0 = pl.multiple_of(jnp.maximum(2 * c - qi + 3, 0) * tq, tq)
        mx = []
        for h in range(n_heads):
            logits = logits_all[:, h * tq:(h + 1) * tq] + nb_s[h, pl.ds(b0, KV_CHUNK), :]
            logits = jnp.where(sel, logits, NEG_BIG)
            lg_s[slot, h] = logits
            mx.append(jnp.max(logits, axis=0, keepdims=True))
        return tuple(mx)

    def softmax_update(c, slot, mx, m_old, l_old):
        k0 = pl.multiple_of(c * KV_CHUNK, KV_CHUNK)
        m_out, l_out, alphas, ps = [], [], [], []
        for h in range(n_heads):
            m_new = jnp.maximum(m_old[h], mx[h])
            alpha = jnp.exp2(m_old[h] - m_new)
            p = jnp.exp2(lg_s[slot, h] - m_new)
            l_out.append(alpha * l_old[h] + jnp.sum(p, axis=0, keepdims=True))
            m_out.append(m_new)
            alphas.append(alpha)
            ps.append(p.astype(BF16))
        acc_s[...] = (jnp.concatenate(alphas, axis=1) * acc_s[...]
                      + _dot(ckvt_s[:, pl.ds(k0, KV_CHUNK)], jnp.concatenate(ps, axis=1)))
        return tuple(m_out), tuple(l_out)

    def attend(c, carry):
        mx, m_old, l_old = carry
        m_new, l_new = softmax_update(c, c & 1, mx, m_old, l_old)
        return masked_logits(c + 1, (c + 1) & 1), m_new, l_new

    carry = (masked_logits(0, 0),
             tuple(jnp.full((1, tq), M_INIT, F32) for _ in range(n_heads)),
             tuple(jnp.zeros((1, tq), F32) for _ in range(n_heads)))
    carry = lax.fori_loop(0, c_last, attend, carry)
    _, l_fin = softmax_update(c_last, c_last & 1, *carry)
    outs = [_dot(wuv_ref[h], (acc_s[:, h * tq:(h + 1) * tq] / l_fin[h]).astype(BF16)) for h in range(n_heads)]
    o = jnp.concatenate(outs, axis=0).T
    out_ref[0] = _rms(o, g_ref[...]).astype(BF16)


def _dsa(rel_bias, p_aq, p_akv, p_misc, qnw, kvnw, wq, wqi, wuk_t, wuv_h, g_a):
    bsz, s, cq = p_aq.shape
    ckv = p_akv.shape[2]
    d_idx = wqi.shape[1] // IDX_HEADS
    topk = min(INDEX_TOPK, s // 4)
    tq = Q_BLOCK
    assert s % KV_CHUNK == 0 and s % tq == 0
    cw = wq.shape[1]
    full2 = lambda a: pl.BlockSpec(a.shape, lambda b, i: (0, 0))
    full3 = lambda a: pl.BlockSpec(a.shape, lambda b, i: (0, 0, 0))
    return pl.pallas_call(
        functools.partial(_dsa_kernel, topk=topk),
        grid=(bsz, s // tq),
        in_specs=[
            pl.BlockSpec(memory_space=pltpu.SMEM),
            pl.BlockSpec((1, tq, cq), lambda b, i: (b, i, 0)),
            pl.BlockSpec((1, s, ckv), lambda b, i: (b, 0, 0)),
            pl.BlockSpec((1, s, LANE), lambda b, i: (b, 0, 0)),
            pl.BlockSpec((1, tq, LANE), lambda b, i: (b, i, 0)),
            full2(qnw), full2(kvnw), full2(wq), full2(wqi), full3(wuk_t), full3(wuv_h), full2(g_a),
        ],
        out_specs=pl.BlockSpec((1, tq, cw), lambda b, i: (b, i, 0)),
        out_shape=jax.ShapeDtypeStruct((bsz, s, cw), BF16),
        scratch_shapes=[
            pltpu.VMEM((s, ckv), BF16),
            pltpu.VMEM((ckv, s), BF16),
            pltpu.VMEM((s, d_idx), BF16),
            pltpu.VMEM((s, tq), I32),
            pltpu.VMEM((32, s // 32, tq), I32),
            pltpu.VMEM((ATTN_HEADS, 5 * tq, tq), F32),
            pltpu.VMEM((ckv, ATTN_HEADS * tq), F32),
            pltpu.VMEM((2, ATTN_HEADS, KV_CHUNK, tq), F32),
        ],
        compiler_params=_params(("arbitrary", "arbitrary"), 48),
        name="dsa",
    )(rel_bias, p_aq, p_akv, p_misc, p_misc, qnw, kvnw, wq, wqi, wuk_t, wuv_h, g_a)


def _out_mlp_kernel(x_ref, yl_ref, ym_ref, ya_ref, mod_ref, wo_ref, n2_ref, w1_ref, w2_ref, fw_ref, out_ref,
                    h2_s, acc_s, *, final):
    j = pl.program_id(2)

    @pl.when(j == 0)
    def _():
        cat = jnp.concatenate([yl_ref[0], ym_ref[0], ya_ref[0]], axis=1)
        x1 = x_ref[0] + mod_ref[2, 0] * _dot(cat, wo_ref[...])
        out_ref[0] = x1
        h2_s[...] = (_rms(x1, n2_ref[...]) * (1.0 + mod_ref[4, 0]) + mod_ref[3, 0]).astype(BF16)
        acc_s[...] = jnp.zeros_like(acc_s)

    a = jnp.maximum(_dot(h2_s[...], w1_ref[...]), 0.0)
    acc_s[...] += _dot((a * a).astype(BF16), w2_ref[...])

    @pl.when(j == pl.num_programs(2) - 1)
    def _():
        x2 = out_ref[0] + mod_ref[5, 0] * acc_s[...]
        if final:
            x2 = _rms(x2, fw_ref[...])
        out_ref[0] = x2


def _out_mlp(x, y_lru, y_m, y_a, mod, wo, n2w, w1, w2, fw, final):
    bsz, s, d = x.shape
    dff = w1.shape[1]
    t = min(MLP_TILE, s)
    fc = min(FF_CHUNK, dff)
    tok = lambda c: pl.BlockSpec((1, t, c), lambda b, i, j: (b, i, 0))
    return pl.pallas_call(
        functools.partial(_out_mlp_kernel, final=final),
        grid=(bsz, s // t, dff // fc),
        in_specs=[
            tok(d), tok(y_lru.shape[2]), tok(y_m.shape[2]), tok(y_a.shape[2]),
            pl.BlockSpec((N_ADA, 1, 1, d), lambda b, i, j: (0, b, 0, 0)),
            pl.BlockSpec((d, d), lambda b, i, j: (0, 0)),
            pl.BlockSpec((1, d), lambda b, i, j: (0, 0)),
            pl.BlockSpec((d, fc), lambda b, i, j: (0, j)),
            pl.BlockSpec((fc, d), lambda b, i, j: (j, 0)),
            pl.BlockSpec((1, d), lambda b, i, j: (0, 0)),
        ],
        out_specs=tok(d),
        out_shape=jax.ShapeDtypeStruct((bsz, s, d), F32),
        scratch_shapes=[pltpu.VMEM((t, d), BF16), pltpu.VMEM((t, d), F32)],
        compiler_params=_params(("parallel", "parallel", "arbitrary"), 56),
        name="out_mlp",
    )(x, y_lru, y_m, y_a, mod, wo, n2w, w1, w2, fw)


def _block_diag(w):
    nb, bi, bo = w.shape
    out = jnp.zeros((nb * bi, nb * bo), w.dtype)
    for n in range(nb):
        out = out.at[n * bi:(n + 1) * bi, n * bo:(n + 1) * bo].set(w[n])
    return out


def kernel(x, c, w_in, conv_w, conv_b, lru_wa, lru_ba, lru_wx, lru_bx, lru_lambda, mlstm_bi, mlstm_bf, w_q_up,
           w_qidx_up, w_uk, w_uv, q_lat_norm_w, kv_lat_norm_w, rel_bias, group_norm_w, w_o, w_ada, b_ada, norm1_w,
           norm2_w, w_mlp1, w_mlp2, final_norm_w):
    depth, d, _ = w_in.shape
    bsz = x.shape[0]
    cl = conv_w.shape[2]
    cm = d // 2
    cq = w_q_up.shape[1]
    ckv = w_uk.shape[1]
    d_idx = w_qidx_up.shape[3]
    nh = MLSTM_HEADS
    dims = (cl, cm, cq, ckv)

    mod_all = _ada(c, w_ada, b_ada).reshape(depth, N_ADA, bsz, 1, d)

    o_gate = 2 * cl + 4 * cm
    o_aq = o_gate + 2 * nh
    o_akv = o_aq + cq
    o_ik = o_akv + ckv
    o_iw = o_ik + d_idx
    zeros = lambda n: jnp.zeros((depth, d, n), w_in.dtype)
    w_in_p = jnp.concatenate([
        w_in[:, :, :o_gate], w_in[:, :, o_aq:o_akv], w_in[:, :, o_akv:o_ik],
        w_in[:, :, o_gate:o_aq], w_in[:, :, o_iw:o_iw + IDX_HEADS], zeros(32 - 2 * nh - IDX_HEADS),
        w_in[:, :, o_ik:o_iw], zeros(LANE - 32 - d_idx),
    ], axis=2).astype(BF16)

    gate_bias = jnp.concatenate([mlstm_bi, mlstm_bf, jnp.zeros((depth, LANE - 2 * nh), F32)], axis=1)
    fw = final_norm_w.reshape(1, d)

    for l in range(depth):
        mod = mod_all[l]
        p_lru, p_qkv, p_mo, p_aq, p_akv, p_misc = _inproj(x, mod, norm1_w[l].reshape(1, d), w_in_p[l], dims)
        gw = group_norm_w[l]
        y_lru = _lru(p_lru, conv_w[l], conv_b[l].reshape(1, cl),
                     _block_diag(lru_wa[l]).astype(BF16), lru_ba[l].reshape(1, cl),
                     _block_diag(lru_wx[l]).astype(BF16), lru_bx[l].reshape(1, cl),
                     lru_lambda[l].reshape(1, cl), gw[:cl].reshape(1, cl))
        y_m = _mlstm(p_qkv, p_mo, p_misc, gate_bias[l].reshape(1, LANE), gw[cl:cl + cm].reshape(1, cm))
        wq = w_q_up[l].reshape(cq, -1).T.astype(BF16)
        wqi = w_qidx_up[l].reshape(cq, -1).T.astype(BF16)
        wuk_t = jnp.transpose(w_uk[l], (1, 0, 2)).astype(BF16)
        wuv_h = jnp.transpose(w_uv[l], (1, 2, 0)).astype(BF16)
        y_a = _dsa(rel_bias, p_aq, p_akv, p_misc, q_lat_norm_w[l].reshape(1, cq), kv_lat_norm_w[l].reshape(1, ckv),
                   wq, wqi, wuk_t, wuv_h, gw[cl + cm:].reshape(1, -1))
        x = _out_mlp(x, y_lru, y_m, y_a, mod, w_o[l].astype(BF16), norm2_w[l].reshape(1, d),
                     w_mlp1[l].astype(BF16), w_mlp2[l].astype(BF16), fw, final=(l == depth - 1))
    return x
```

```python
import functools
import math

import jax
import jax.numpy as jnp
from jax import lax
from jax.experimental import pallas as pl
from jax.experimental.pallas import tpu as pltpu

F32 = jnp.float32
BF16 = jnp.bfloat16
I32 = jnp.int32

NORM_EPS = 1e-6
CONV_WIDTH = 4
LRU_C = 8.0
MLSTM_HEADS = 4
GATE_SOFTCAP = 15.0
ATTN_HEADS = 4
IDX_HEADS = 8
INDEX_TOPK = 256
REL_BUCKETS = 32
REL_MAX_EXACT = 16
REL_MAX_DIST = 128
N_ADA = 6

LANE = 128
Q_BLOCK = 256
KV_CHUNK = 2 * Q_BLOCK
BIAS_ROWS = 64
WORD_ROWS = 256
SCORE_ROWS = 128
MLSTM_CHUNK = 256
MLSTM_STEP_CHUNKS = 4
TOKEN_TILE = 512
LRU_STEP_TILES = 4
MLP_TILE = 1024
FF_CHUNK = 1024
INT_MIN = -(2 ** 31)
INT_MAX = 2 ** 31 - 1
NEG_BIG = -1e30
M_INIT = -1e29
LOG2E = math.log2(math.e)
MIB = 1024 * 1024


def _rms(x, w):
    return x * lax.rsqrt(jnp.mean(x * x, axis=-1, keepdims=True) + NORM_EPS) * w


def _sigmoid(x):
    return 1.0 / (1.0 + jnp.exp(-x))


def _dot(a, b):
    return jnp.dot(a, b, preferred_element_type=F32)


def _dot_nt(a, b):
    return lax.dot_general(a, b, (((1,), (1,)), ((), ())), preferred_element_type=F32)


def _bit_transpose32(a):
    a = list(a)
    mask = 0x0000FFFF
    j = 16
    while j:
        k = 0
        while k < 32:
            t = (a[k] ^ lax.shift_right_logical(a[k | j], j)) & mask
            a[k] = a[k] ^ t
            a[k | j] = a[k | j] ^ lax.shift_left(t, j)
            k = ((k | j) + 1) & ~j
        j >>= 1
        mask = (mask ^ (mask << j)) & 0xFFFFFFFF
    return a


def _params(sem, vmem_mib):
    return pltpu.CompilerParams(dimension_semantics=sem, vmem_limit_bytes=vmem_mib * MIB)


def _ada_kernel(c_ref, w_ref, b_ref, o_ref):
    c = c_ref[...]
    act = (c * _sigmoid(c)).astype(BF16)
    o_ref[0, 0] = _dot(act, w_ref[0].astype(BF16)) + b_ref[0, 0]


def _ada(c, w_ada, b_ada):
    depth, d, _ = w_ada.shape
    bsz = c.shape[0]
    b4 = b_ada.reshape(depth, N_ADA, 1, d)
    return pl.pallas_call(
        _ada_kernel,
        grid=(depth, N_ADA),
        in_specs=[
            pl.BlockSpec((bsz, d), lambda l, k: (0, 0)),
            pl.BlockSpec((1, d, d), lambda l, k: (l, 0, k)),
            pl.BlockSpec((1, 1, 1, d), lambda l, k: (l, k, 0, 0)),
        ],
        out_specs=pl.BlockSpec((1, 1, bsz, d), lambda l, k: (l, k, 0, 0)),
        out_shape=jax.ShapeDtypeStruct((depth, N_ADA, bsz, d), F32),
        compiler_params=_params(("arbitrary", "arbitrary"), 32),
        name="ada_mod",
    )(c, w_ada, b4)


def _inproj_kernel(x_ref, mod_ref, nw_ref, w_ref, lru_ref, qkv_ref, mo_ref, aq_ref, akv_ref, misc_ref, *, dims):
    cl, cm, cq, ckv = dims
    x = x_ref[0]
    sh = mod_ref[0, 0]
    sc = mod_ref[1, 0]
    h = _rms(x, nw_ref[...]) * (1.0 + sc) + sh
    proj = _dot(h.astype(BF16), w_ref[...])
    o = 0
    lru_ref[0] = proj[:, o:o + 2 * cl]
    o += 2 * cl
    qkv_ref[0] = proj[:, o:o + 3 * cm].astype(BF16)
    o += 3 * cm
    mo_ref[0] = proj[:, o:o + cm]
    o += cm
    aq_ref[0] = proj[:, o:o + cq]
    o += cq
    akv_ref[0] = proj[:, o:o + ckv]
    o += ckv
    misc_ref[0] = proj[:, o:o + LANE]


def _inproj(x, mod, nw, w_in_p, dims):
    bsz, s, d = x.shape
    cl, cm, cq, ckv = dims
    n_out = w_in_p.shape[1]
    t = min(TOKEN_TILE, s)
    tok = lambda c: pl.BlockSpec((1, t, c), lambda b, i: (b, i, 0))
    return pl.pallas_call(
        functools.partial(_inproj_kernel, dims=dims),
        grid=(bsz, s // t),
        in_specs=[
            tok(d),
            pl.BlockSpec((N_ADA, 1, 1, d), lambda b, i: (0, b, 0, 0)),
            pl.BlockSpec((1, d), lambda b, i: (0, 0)),
            pl.BlockSpec((d, n_out), lambda b, i: (0, 0)),
        ],
        out_specs=[tok(2 * cl), tok(3 * cm), tok(cm), tok(cq), tok(ckv), tok(LANE)],
        out_shape=[
            jax.ShapeDtypeStruct((bsz, s, 2 * cl), F32),
            jax.ShapeDtypeStruct((bsz, s, 3 * cm), BF16),
            jax.ShapeDtypeStruct((bsz, s, cm), F32),
            jax.ShapeDtypeStruct((bsz, s, cq), F32),
            jax.ShapeDtypeStruct((bsz, s, ckv), F32),
            jax.ShapeDtypeStruct((bsz, s, LANE), F32),
        ],
        compiler_params=_params(("parallel", "parallel"), 56),
        name="in_proj",
    )(x, mod, nw, w_in_p)


def _lru_kernel(p_ref, cw_ref, cb_ref, wa_ref, ba_ref, wx_ref, bx_ref, lam_ref, g_ref, out_ref, hc_ref, xt_ref):
    t_len = min(TOKEN_TILE, p_ref.shape[1])

    @pl.when(pl.program_id(1) == 0)
    def _():
        hc_ref[...] = jnp.zeros_like(hc_ref)
        xt_ref[...] = jnp.zeros_like(xt_ref)

    lax.fori_loop(0, p_ref.shape[1] // t_len,
                  functools.partial(_lru_tile, p_ref, cw_ref, cb_ref, wa_ref, ba_ref, wx_ref, bx_ref, lam_ref, g_ref,
                                    out_ref, hc_ref, xt_ref, t_len), 0)


def _lru_tile(p_ref, cw_ref, cb_ref, wa_ref, ba_ref, wx_ref, bx_ref, lam_ref, g_ref, out_ref, hc_ref, xt_ref, t_len,
              ti, carry):
    c = p_ref.shape[2] // 2
    rows = pl.ds(pl.multiple_of(ti * t_len, t_len), t_len)
    p = p_ref[0, rows, :]
    x = p[:, :c]
    y = p[:, c:]
    cw = cw_ref[...]
    xe = jnp.concatenate([xt_ref[...], x], axis=0)
    xc = cb_ref[...] + cw[CONV_WIDTH - 1:CONV_WIDTH] * x
    for j in range(1, CONV_WIDTH):
        xc = xc + cw[CONV_WIDTH - 1 - j:CONV_WIDTH - j] * pltpu.roll(xe, j, axis=0)[8:]
    xt_ref[...] = x[t_len - 8:]

    xb = xc.astype(BF16)
    r = _sigmoid(_dot(xb, wa_ref[...]) + ba_ref[...])
    gi = _sigmoid(_dot(xb, wx_ref[...]) + bx_ref[...])
    nl = -lam_ref[...]
    softplus = jnp.maximum(nl, 0.0) + jnp.log(1.0 + jnp.exp(-jnp.abs(nl)))
    log_a = -LRU_C * r * softplus
    a = jnp.exp(log_a)
    u = jnp.sqrt(1.0 - jnp.exp(2.0 * log_a)) * (gi * xc)

    row = lax.broadcasted_iota(I32, (t_len, c), 0)
    d = 1
    while d < t_len:
        keep = row >= d
        a_s = jnp.where(keep, pltpu.roll(a, d, axis=0), 1.0)
        u_s = jnp.where(keep, pltpu.roll(u, d, axis=0), 0.0)
        u = a * u_s + u
        a = a * a_s
        d *= 2
    h = a * hc_ref[...] + u
    hc_ref[...] = h[t_len - 1:]

    gelu = 0.5 * y * (1.0 + jnp.tanh(math.sqrt(2.0 / math.pi) * (y + 0.044715 * (y * y * y))))
    out_ref[0, rows, :] = _rms(gelu * h, g_ref[...]).astype(BF16)
    return carry


def _lru(p_lru, conv_w, conv_b, wa_bd, ba, wx_bd, bx, lam, g_lru):
    bsz, s, c2 = p_lru.shape
    c = c2 // 2
    t = min(TOKEN_TILE * LRU_STEP_TILES, s)
    row = lambda n: pl.BlockSpec((1, n), lambda b, i: (0, 0))
    full = lambda a: pl.BlockSpec(a.shape, lambda b, i: (0, 0))
    return pl.pallas_call(
        _lru_kernel,
        grid=(bsz, s // t),
        in_specs=[
            pl.BlockSpec((1, t, c2), lambda b, i: (b, i, 0)),
            full(conv_w), row(c), full(wa_bd), row(c), full(wx_bd), row(c), row(c), row(c),
        ],
        out_specs=pl.BlockSpec((1, t, c), lambda b, i: (b, i, 0)),
        out_shape=jax.ShapeDtypeStruct((bsz, s, c), BF16),
        scratch_shapes=[pltpu.VMEM((1, c), F32), pltpu.VMEM((8, c), F32)],
        compiler_params=_params(("arbitrary", "arbitrary"), 32),
        name="rg_lru",
    )(p_lru, conv_w, conv_b, wa_bd, ba, wx_bd, bx, lam, g_lru)


def _mlstm_kernel(qkv_ref, mo_ref, misc_ref, gb_ref, nw_ref, out_ref, ct_ref, m_ref):
    l_len = min(MLSTM_CHUNK, qkv_ref.shape[1])

    @pl.when(pl.program_id(1) == 0)
    def _():
        ct_ref[...] = jnp.zeros_like(ct_ref)
        m_ref[...] = jnp.zeros_like(m_ref)

    lax.fori_loop(0, qkv_ref.shape[1] // l_len,
                  functools.partial(_mlstm_chunk, qkv_ref, mo_ref, misc_ref, gb_ref, nw_ref, out_ref, ct_ref, m_ref,
                                    l_len), 0)


def _mlstm_chunk(qkv_ref, mo_ref, misc_ref, gb_ref, nw_ref, out_ref, ct_ref, m_ref, l_len, ci, carry):
    dh = ct_ref.shape[2]
    nh = ct_ref.shape[0]
    rows = pl.ds(pl.multiple_of(ci * l_len, l_len), l_len)
    g = misc_ref[0, rows, :] + gb_ref[...]
    capped = GATE_SOFTCAP * jnp.tanh(g / GATE_SOFTCAP)
    logf = jnp.minimum(capped, 0.0) - jnp.log(1.0 + jnp.exp(-jnp.abs(capped)))
    row = lax.broadcasted_iota(I32, (l_len, LANE), 0)
    bcum = logf
    d = 1
    while d < l_len:
        bcum = bcum + jnp.where(row >= d, pltpu.roll(bcum, d, axis=0), 0.0)
        d *= 2
    capped_t = capped.T
    bcum_t = bcum.T
    causal = (lax.broadcasted_iota(I32, (l_len, l_len), 0) <= lax.broadcasted_iota(I32, (l_len, l_len), 1))
    ones_rows = jnp.ones((8, l_len), F32)

    for h in range(nh):
        q = qkv_ref[0, rows, h * dh:(h + 1) * dh]
        k = qkv_ref[0, rows, (nh + h) * dh:(nh + h + 1) * dh]
        v = qkv_ref[0, rows, (2 * nh + h) * dh:(2 * nh + h + 1) * dh]
        qs = (q.astype(F32) * (dh ** -0.5)).astype(BF16)
        b_row = bcum_t[nh + h:nh + h + 1, :]
        i_row = capped_t[h:h + 1, :]
        r_col = capped[:, h:h + 1] - bcum[:, nh + h:nh + h + 1]
        m_prev = m_ref[h][:, 0:1]
        ct_old = ct_ref[h]

        dmat = jnp.where(causal, r_col + b_row, -jnp.inf)
        inter = b_row + m_prev
        m_row = jnp.maximum(inter, jnp.max(dmat, axis=0, keepdims=True))
        inter_w = jnp.exp(inter - m_row)
        smat = _dot_nt(k, qs) * jnp.exp(dmat - m_row)
        v_t = v.astype(F32).T
        cq = _dot_nt(ct_old.astype(BF16), qs)
        num = _dot(v_t.astype(BF16), smat.astype(BF16)) + inter_w * cq[:dh]
        den = jnp.sum(smat, axis=0, keepdims=True) + inter_w * cq[dh:dh + 1]
        hh = num / jnp.maximum(jnp.abs(den), jnp.exp(-m_row))

        b_last = b_row[:, l_len - 1:]
        w_row = b_last - b_row + i_row
        m_new = jnp.maximum(b_last + m_prev, jnp.max(w_row, axis=1, keepdims=True))
        decay = jnp.exp(b_last + m_prev - m_new)
        vw = jnp.concatenate([v_t, ones_rows], axis=0) * jnp.exp(w_row - m_new)
        ct_ref[h] = decay * ct_old + _dot(vw.astype(BF16), k)
        m_ref[h] = jnp.broadcast_to(m_new, (1, LANE))

        hn_t = hh * lax.rsqrt(jnp.mean(hh * hh, axis=0, keepdims=True) + NORM_EPS)
        hn = hn_t.T * nw_ref[:, h * dh:(h + 1) * dh]
        gate = _sigmoid(mo_ref[0, rows, h * dh:(h + 1) * dh])
        out_ref[0, rows, h * dh:(h + 1) * dh] = (gate * hn).astype(BF16)
    return carry


def _mlstm(p_qkv, p_mo, p_misc, gate_bias, g_m):
    bsz, s, cm = p_mo.shape
    nh = MLSTM_HEADS
    dh = cm // nh
    l_len = min(MLSTM_CHUNK * MLSTM_STEP_CHUNKS, s)
    tok = lambda c: pl.BlockSpec((1, l_len, c), lambda b, i: (b, i, 0))
    return pl.pallas_call(
        _mlstm_kernel,
        grid=(bsz, s // l_len),
        in_specs=[
            tok(3 * cm), tok(cm), tok(LANE),
            pl.BlockSpec((1, LANE), lambda b, i: (0, 0)),
            pl.BlockSpec((1, cm), lambda b, i: (0, 0)),
        ],
        out_specs=tok(cm),
        out_shape=jax.ShapeDtypeStruct((bsz, s, cm), BF16),
        scratch_shapes=[pltpu.VMEM((nh, dh + 8, dh), F32), pltpu.VMEM((nh, 1, LANE), F32)],
        compiler_params=_params(("arbitrary", "arbitrary"), 32),
        name="mlstm",
    )(p_qkv, p_mo, p_misc, gate_bias, g_m)


def _dsa_kernel(rb_ref, aq_ref, akv_ref, misck_ref, miscq_ref, qnw_ref, kvnw_ref, wq_ref, wqi_ref, wuk_ref, wuv_ref,
                g_ref, out_ref, ckv_s, ckvt_s, ik_s, key_s, bits_s, nb_s, acc_s, lg_s, *, topk):
    qi = pl.program_id(1)
    tq = aq_ref.shape[1]
    n_heads = wuk_ref.shape[0]
    dh = wuk_ref.shape[2]
    d_idx = ik_s.shape[1]

    @pl.when(qi == 0)
    def _():
        ckv = _rms(akv_ref[0], kvnw_ref[...])
        ckv_s[...] = ckv.astype(BF16)
        ckvt_s[...] = ckv.T.astype(BF16)
        ik_s[...] = misck_ref[0][:, 32:32 + d_idx].astype(BF16)
        bits_s[...] = jnp.zeros_like(bits_s)

    @pl.when((pl.program_id(0) == 0) & (qi == 0))
    def _():
        def build(rb, carry):
            r0 = pl.multiple_of(rb * BIAS_ROWS, BIAS_ROWS)
            rr = lax.broadcasted_iota(I32, (BIAS_ROWS, tq), 0) + r0
            ii = lax.broadcasted_iota(I32, (BIAS_ROWS, tq), 1)
            n = jnp.maximum(ii + 3 * tq - rr, 0)
            log_ratio = jnp.log(jnp.maximum(n, 1).astype(F32) / REL_MAX_EXACT) / math.log(REL_MAX_DIST / REL_MAX_EXACT)
            large = jnp.minimum(REL_MAX_EXACT + (log_ratio * (REL_BUCKETS - REL_MAX_EXACT)).astype(I32),
                                REL_BUCKETS - 1)
            bucket = jnp.where(n < REL_MAX_EXACT, n, large)
            for h in range(n_heads):
                last = rb_ref[REL_BUCKETS - 1, h]
                val = jnp.zeros((BIAS_ROWS, tq), F32)
                for kb in range(REL_BUCKETS - 1):
                    val = jnp.where(bucket == kb, (rb_ref[kb, h] - last) * LOG2E, val)
                nb_s[h, pl.ds(r0, BIAS_ROWS), :] = val
            return carry

        lax.fori_loop(0, nb_s.shape[1] // BIAS_ROWS, build, 0)

    q_lat_t = _rms(aq_ref[0], qnw_ref[...]).T.astype(BF16)
    q_t = _dot(wq_ref[...], q_lat_t).astype(BF16)
    qa_all_t = jnp.concatenate(
        [(_dot(wuk_ref[h], q_t[h * dh:(h + 1) * dh]) * (dh ** -0.5 * LOG2E)).astype(BF16) for h in range(n_heads)],
        axis=1)
    q_idx_t = _dot(wqi_ref[...], q_lat_t) * (d_idx ** -0.5)
    q_idx_all = jnp.concatenate([q_idx_t[h * d_idx:(h + 1) * d_idx] for h in range(IDX_HEADS)], axis=1).astype(BF16)
    w_i = miscq_ref[0].T[8:8 + IDX_HEADS] * (IDX_HEADS ** -0.5)

    t_pos = qi * tq + lax.broadcasted_iota(I32, (SCORE_ROWS, tq), 1)
    row = lax.broadcasted_iota(I32, (SCORE_ROWS, tq), 0)

    def score_group(g):
        for sub in range(WORD_ROWS // SCORE_ROWS):
            r0 = pl.multiple_of(g * WORD_ROWS + sub * SCORE_ROWS, SCORE_ROWS)
            d_all = _dot(ik_s[pl.ds(r0, SCORE_ROWS), :], q_idx_all)
            sc = jnp.zeros((SCORE_ROWS, tq), F32)
            for h in range(IDX_HEADS):
                sc = sc + jnp.maximum(d_all[:, h * tq:(h + 1) * tq], 0.0) * w_i[h:h + 1]
            bits = lax.bitcast_convert_type(sc, I32)
            key = jnp.where(bits < 0, bits ^ INT_MAX, bits)
            key_s[pl.ds(r0, SCORE_ROWS), :] = jnp.where(row + r0 <= t_pos, key, INT_MIN)

    lane_tiles = tq // LANE

    def slice_group(g):
        k0 = pl.multiple_of(g * WORD_ROWS, WORD_ROWS)
        w0 = pl.multiple_of(g * 8, 8)
        for lt in range(lane_tiles):
            cols = slice(lt * LANE, (lt + 1) * LANE)
            w = _bit_transpose32([key_s[pl.ds(k0 + j * 8, 8), cols] ^ INT_MIN for j in range(32)])
            for i in range(32):
                bits_s[31 - i, pl.ds(w0, 8), cols] = w[i]

    def score_and_slice(g, carry):
        slice_group(g - 1)
        score_group(g)
        return carry

    n_word_groups = (qi + 1) * (tq // WORD_ROWS)

    @pl.when((qi & 1) == 0)
    def _():
        key_s[pl.ds(pl.multiple_of((qi + 1) * tq, tq), tq), :] = jnp.full((tq, tq), INT_MIN, I32)

    score_group(0)
    lax.fori_loop(1, n_word_groups, score_and_slice, 0)
    slice_group(n_word_groups - 1)

    n_word_rows = bits_s.shape[1]
    wrow = lax.broadcasted_iota(I32, (n_word_rows, tq), 0)
    first_key = lax.shift_right_logical(wrow, 3) * WORD_ROWS + (wrow & 7)
    t_lane = qi * tq + lax.broadcasted_iota(I32, (n_word_rows, tq), 1)
    n_valid = jnp.clip(lax.shift_right_arithmetic(t_lane - first_key, 3) + 1, 0, 32)
    eq0 = jnp.where(n_valid <= 0, 0, lax.shift_left(jnp.int32(-1), 32 - jnp.maximum(n_valid, 1)))

    def bit_step(i, carry):
        eq, c_gt, ans = carry
        b = 31 - i
        t = eq & bits_s[b]
        cnt = jnp.sum(lax.population_count(t), axis=0, keepdims=True)
        take = (c_gt + cnt) >= topk
        eq = jnp.where(take, t, eq ^ t)
        c_gt = jnp.where(take, c_gt, c_gt + cnt)
        ans = jnp.where(take, ans | lax.shift_left(jnp.int32(1), b), ans)
        return eq, c_gt, ans

    zero_row = jnp.zeros((1, tq), I32)
    _, _, ans = lax.fori_loop(0, 32, bit_step, (eq0, zero_row, zero_row))
    thr = jnp.maximum(ans ^ INT_MIN, INT_MIN + 1)

    c_last = lax.shift_right_logical(qi, 1)
    acc_s[...] = jnp.zeros_like(acc_s)

    def masked_logits(c, slot):
        k0 = pl.multiple_of(c * KV_CHUNK, KV_CHUNK)
        sel = key_s[pl.ds(k0, KV_CHUNK), :] >= thr
        logits_all = _dot(ckv_s[pl.ds(k0, KV_CHUNK), :], qa_all_t)
        b0 = pl.multiple_of(jnp.maximum(2 * c - qi + 3, 0) * tq, tq)
        mx = []
        for h in range(n_heads):
            logits = logits_all[:, h * tq:(h + 1) * tq] + nb_s[h, pl.ds(b0, KV_CHUNK), :]
            logits = jnp.where(sel, logits, NEG_BIG)
            lg_s[slot, h] = logits
            mx.append(jnp.max(logits, axis=0, keepdims=True))
        return tuple(mx)

    def softmax_update(c, slot, mx, m_old, l_old):
        k0 = pl.multiple_of(c * KV_CHUNK, KV_CHUNK)
        m_out, l_out, alphas, ps = [], [], [], []
        for h in range(n_heads):
            m_new = jnp.maximum(m_old[h], mx[h])
            alpha = jnp.exp2(m_old[h] - m_new)
            p = jnp.exp2(lg_s[slot, h] - m_new)
            l_out.append(alpha * l_old[h] + jnp.sum(p, axis=0, keepdims=True))
            m_out.append(m_new)
            alphas.append(alpha)
            ps.append(p.astype(BF16))
        acc_s[...] = (jnp.concatenate(alphas, axis=1) * acc_s[...]
                      + _dot(ckvt_s[:, pl.ds(k0, KV_CHUNK)], jnp.concatenate(ps, axis=1)))
        return tuple(m_out), tuple(l_out)

    def attend(c, carry):
        mx, m_old, l_old = carry
        m_new, l_new = softmax_update(c, c & 1, mx, m_old, l_old)
        return masked_logits(c + 1, (c + 1) & 1), m_new, l_new

    carry = (masked_logits(0, 0),
             tuple(jnp.full((1, tq), M_INIT, F32) for _ in range(n_heads)),
             tuple(jnp.zeros((1, tq), F32) for _ in range(n_heads)))
    carry = lax.fori_loop(0, c_last, attend, carry)
    _, l_fin = softmax_update(c_last, c_last & 1, *carry)
    outs = [_dot(wuv_ref[h], (acc_s[:, h * tq:(h + 1) * tq] / l_fin[h]).astype(BF16)) for h in range(n_heads)]
    o = jnp.concatenate(outs, axis=0).T
    out_ref[0] = _rms(o, g_ref[...]).astype(BF16)


def _dsa(rel_bias, p_aq, p_akv, p_misc, qnw, kvnw, wq, wqi, wuk_t, wuv_h, g_a):
    bsz, s, cq = p_aq.shape
    ckv = p_akv.shape[2]
    d_idx = wqi.shape[0] // IDX_HEADS
    topk = min(INDEX_TOPK, s // 4)
    tq = Q_BLOCK
    assert s % KV_CHUNK == 0 and tq % WORD_ROWS == 0 and WORD_ROWS % SCORE_ROWS == 0
    cw = wq.shape[0]
    full2 = lambda a: pl.BlockSpec(a.shape, lambda b, i: (0, 0))
    full3 = lambda a: pl.BlockSpec(a.shape, lambda b, i: (0, 0, 0))
    return pl.pallas_call(
        functools.partial(_dsa_kernel, topk=topk),
        grid=(bsz, s // tq),
        in_specs=[
            pl.BlockSpec(memory_space=pltpu.SMEM),
            pl.BlockSpec((1, tq, cq), lambda b, i: (b, i, 0)),
            pl.BlockSpec((1, s, ckv), lambda b, i: (b, 0, 0)),
            pl.BlockSpec((1, s, LANE), lambda b, i: (b, 0, 0)),
            pl.BlockSpec((1, tq, LANE), lambda b, i: (b, i, 0)),
            full2(qnw), full2(kvnw), full2(wq), full2(wqi), full3(wuk_t), full3(wuv_h), full2(g_a),
        ],
        out_specs=pl.BlockSpec((1, tq, cw), lambda b, i: (b, i, 0)),
        out_shape=jax.ShapeDtypeStruct((bsz, s, cw), BF16),
        scratch_shapes=[
            pltpu.VMEM((s, ckv), BF16),
            pltpu.VMEM((ckv, s), BF16),
            pltpu.VMEM((s, d_idx), BF16),
            pltpu.VMEM((s, tq), I32),
            pltpu.VMEM((32, s // 32, tq), I32),
            pltpu.VMEM((ATTN_HEADS, 5 * tq, tq), F32),
            pltpu.VMEM((ckv, ATTN_HEADS * tq), F32),
            pltpu.VMEM((2, ATTN_HEADS, KV_CHUNK, tq), F32),
        ],
        compiler_params=_params(("arbitrary", "arbitrary"), 48),
        name="dsa",
    )(rel_bias, p_aq, p_akv, p_misc, p_misc, qnw, kvnw, wq, wqi, wuk_t, wuv_h, g_a)


def _out_mlp_kernel(x_ref, yl_ref, ym_ref, ya_ref, mod_ref, wo_ref, n2_ref, w1_ref, w2_ref, fw_ref, out_ref,
                    h2_s, acc_s, *, final):
    j = pl.program_id(2)

    @pl.when(j == 0)
    def _():
        cat = jnp.concatenate([yl_ref[0], ym_ref[0], ya_ref[0]], axis=1)
        x1 = x_ref[0] + mod_ref[2, 0] * _dot(cat, wo_ref[...])
        out_ref[0] = x1
        h2_s[...] = (_rms(x1, n2_ref[...]) * (1.0 + mod_ref[4, 0]) + mod_ref[3, 0]).astype(BF16)
        acc_s[...] = jnp.zeros_like(acc_s)

    a = jnp.maximum(_dot(h2_s[...], w1_ref[...]), 0.0)
    acc_s[...] += _dot((a * a).astype(BF16), w2_ref[...])

    @pl.when(j == pl.num_programs(2) - 1)
    def _():
        x2 = out_ref[0] + mod_ref[5, 0] * acc_s[...]
        if final:
            x2 = _rms(x2, fw_ref[...])
        out_ref[0] = x2


def _out_mlp(x, y_lru, y_m, y_a, mod, wo, n2w, w1, w2, fw, final):
    bsz, s, d = x.shape
    dff = w1.shape[1]
    t = min(MLP_TILE, s)
    fc = min(FF_CHUNK, dff)
    tok = lambda c: pl.BlockSpec((1, t, c), lambda b, i, j: (b, i, 0))
    return pl.pallas_call(
        functools.partial(_out_mlp_kernel, final=final),
        grid=(bsz, s // t, dff // fc),
        in_specs=[
            tok(d), tok(y_lru.shape[2]), tok(y_m.shape[2]), tok(y_a.shape[2]),
            pl.BlockSpec((N_ADA, 1, 1, d), lambda b, i, j: (0, b, 0, 0)),
            pl.BlockSpec((d, d), lambda b, i, j: (0, 0)),
            pl.BlockSpec((1, d), lambda b, i, j: (0, 0)),
            pl.BlockSpec((d, fc), lambda b, i, j: (0, j)),
            pl.BlockSpec((fc, d), lambda b, i, j: (j, 0)),
            pl.BlockSpec((1, d), lambda b, i, j: (0, 0)),
        ],
        out_specs=tok(d),
        out_shape=jax.ShapeDtypeStruct((bsz, s, d), F32),
        scratch_shapes=[pltpu.VMEM((t, d), BF16), pltpu.VMEM((t, d), F32)],
        compiler_params=_params(("parallel", "parallel", "arbitrary"), 56),
        name="out_mlp",
    )(x, y_lru, y_m, y_a, mod, wo, n2w, w1, w2, fw)


def _block_diag(w):
    nb, bi, bo = w.shape
    out = jnp.zeros((nb * bi, nb * bo), w.dtype)
    for n in range(nb):
        out = out.at[n * bi:(n + 1) * bi, n * bo:(n + 1) * bo].set(w[n])
    return out


def kernel(x, c, w_in, conv_w, conv_b, lru_wa, lru_ba, lru_wx, lru_bx, lru_lambda, mlstm_bi, mlstm_bf, w_q_up,
           w_qidx_up, w_uk, w_uv, q_lat_norm_w, kv_lat_norm_w, rel_bias, group_norm_w, w_o, w_ada, b_ada, norm1_w,
           norm2_w, w_mlp1, w_mlp2, final_norm_w):
    depth, d, _ = w_in.shape
    bsz = x.shape[0]
    cl = conv_w.shape[2]
    cm = d // 2
    cq = w_q_up.shape[1]
    ckv = w_uk.shape[1]
    d_idx = w_qidx_up.shape[3]
    nh = MLSTM_HEADS
    dims = (cl, cm, cq, ckv)

    mod_all = _ada(c, w_ada, b_ada).reshape(depth, N_ADA, bsz, 1, d)

    o_gate = 2 * cl + 4 * cm
    o_aq = o_gate + 2 * nh
    o_akv = o_aq + cq
    o_ik = o_akv + ckv
    o_iw = o_ik + d_idx
    zeros = lambda n: jnp.zeros((depth, d, n), w_in.dtype)
    w_in_p = jnp.concatenate([
        w_in[:, :, :o_gate], w_in[:, :, o_aq:o_akv], w_in[:, :, o_akv:o_ik],
        w_in[:, :, o_gate:o_aq], w_in[:, :, o_iw:o_iw + IDX_HEADS], zeros(32 - 2 * nh - IDX_HEADS),
        w_in[:, :, o_ik:o_iw], zeros(LANE - 32 - d_idx),
    ], axis=2).astype(BF16)

    gate_bias = jnp.concatenate([mlstm_bi, mlstm_bf, jnp.zeros((depth, LANE - 2 * nh), F32)], axis=1)
    fw = final_norm_w.reshape(1, d)

    for l in range(depth):
        mod = mod_all[l]
        p_lru, p_qkv, p_mo, p_aq, p_akv, p_misc = _inproj(x, mod, norm1_w[l].reshape(1, d), w_in_p[l], dims)
        gw = group_norm_w[l]
        y_lru = _lru(p_lru, conv_w[l], conv_b[l].reshape(1, cl),
                     _block_diag(lru_wa[l]).astype(BF16), lru_ba[l].reshape(1, cl),
                     _block_diag(lru_wx[l]).astype(BF16), lru_bx[l].reshape(1, cl),
                     lru_lambda[l].reshape(1, cl), gw[:cl].reshape(1, cl))
        y_m = _mlstm(p_qkv, p_mo, p_misc, gate_bias[l].reshape(1, LANE), gw[cl:cl + cm].reshape(1, cm))
        wq = w_q_up[l].reshape(cq, -1).T.astype(BF16)
        wqi = w_qidx_up[l].reshape(cq, -1).T.astype(BF16)
        wuk_t = jnp.transpose(w_uk[l], (1, 0, 2)).astype(BF16)
        wuv_h = jnp.transpose(w_uv[l], (1, 2, 0)).astype(BF16)
        y_a = _dsa(rel_bias, p_aq, p_akv, p_misc, q_lat_norm_w[l].reshape(1, cq), kv_lat_norm_w[l].reshape(1, ckv),
                   wq, wqi, wuk_t, wuv_h, gw[cl + cm:].reshape(1, -1))
        x = _out_mlp(x, y_lru, y_m, y_a, mod, w_o[l].astype(BF16), norm2_w[l].reshape(1, d),
                     w_mlp1[l].astype(BF16), w_mlp2[l].astype(BF16), fw, final=(l == depth - 1))
    return x
```

```python
import functools
import math

import jax
import jax.numpy as jnp
from jax import lax
from jax.experimental import pallas as pl
from jax.experimental.pallas import tpu as pltpu

F32 = jnp.float32
BF16 = jnp.bfloat16
I32 = jnp.int32

NORM_EPS = 1e-6
CONV_WIDTH = 4
LRU_C = 8.0
MLSTM_HEADS = 4
GATE_SOFTCAP = 15.0
ATTN_HEADS = 4
IDX_HEADS = 8
INDEX_TOPK = 256
REL_BUCKETS = 32
REL_MAX_EXACT = 16
REL_MAX_DIST = 128
N_ADA = 6

LANE = 128
Q_BLOCK = 256
KV_CHUNK = 2 * Q_BLOCK
BIAS_ROWS = 64
WORD_ROWS = 256
SCORE_ROWS = 128
MLSTM_CHUNK = 256
MLSTM_STEP_CHUNKS = 4
TOKEN_TILE = 512
LRU_STEP_TILES = 4
MLP_TILE = 1024
FF_CHUNK = 1024
INT_MIN = -(2 ** 31)
INT_MAX = 2 ** 31 - 1
NEG_BIG = -1e30
M_INIT = -1e29
LOG2E = math.log2(math.e)
MIB = 1024 * 1024


def _rms(x, w):
    return x * lax.rsqrt(jnp.mean(x * x, axis=-1, keepdims=True) + NORM_EPS) * w


def _sigmoid(x):
    return 1.0 / (1.0 + jnp.exp(-x))


def _dot(a, b):
    return jnp.dot(a, b, preferred_element_type=F32)


def _dot_nt(a, b):
    return lax.dot_general(a, b, (((1,), (1,)), ((), ())), preferred_element_type=F32)


def _bit_transpose32(a):
    a = list(a)
    mask = 0x0000FFFF
    j = 16
    while j:
        k = 0
        while k < 32:
            t = (a[k] ^ lax.shift_right_logical(a[k | j], j)) & mask
            a[k] = a[k] ^ t
            a[k | j] = a[k | j] ^ lax.shift_left(t, j)
            k = ((k | j) + 1) & ~j
        j >>= 1
        mask = (mask ^ (mask << j)) & 0xFFFFFFFF
    return a


def _params(sem, vmem_mib):
    return pltpu.CompilerParams(dimension_semantics=sem, vmem_limit_bytes=vmem_mib * MIB)


def _ada_kernel(c_ref, w_ref, b_ref, o_ref):
    c = c_ref[...]
    act = (c * _sigmoid(c)).astype(BF16)
    o_ref[0, 0] = _dot(act, w_ref[0].astype(BF16)) + b_ref[0, 0]


def _ada(c, w_ada, b_ada):
    depth, d, _ = w_ada.shape
    bsz = c.shape[0]
    b4 = b_ada.reshape(depth, N_ADA, 1, d)
    return pl.pallas_call(
        _ada_kernel,
        grid=(depth, N_ADA),
        in_specs=[
            pl.BlockSpec((bsz, d), lambda l, k: (0, 0)),
            pl.BlockSpec((1, d, d), lambda l, k: (l, 0, k)),
            pl.BlockSpec((1, 1, 1, d), lambda l, k: (l, k, 0, 0)),
        ],
        out_specs=pl.BlockSpec((1, 1, bsz, d), lambda l, k: (l, k, 0, 0)),
        out_shape=jax.ShapeDtypeStruct((depth, N_ADA, bsz, d), F32),
        compiler_params=_params(("arbitrary", "arbitrary"), 32),
        name="ada_mod",
    )(c, w_ada, b4)


def _inproj_kernel(x_ref, mod_ref, nw_ref, w_ref, lru_ref, qkv_ref, mo_ref, aq_ref, akv_ref, misc_ref, *, dims):
    cl, cm, cq, ckv = dims
    x = x_ref[0]
    sh = mod_ref[0, 0]
    sc = mod_ref[1, 0]
    h = _rms(x, nw_ref[...]) * (1.0 + sc) + sh
    proj = _dot(h.astype(BF16), w_ref[...])
    o = 0
    lru_ref[0] = proj[:, o:o + 2 * cl]
    o += 2 * cl
    qkv_ref[0] = proj[:, o:o + 3 * cm].astype(BF16)
    o += 3 * cm
    mo_ref[0] = proj[:, o:o + cm]
    o += cm
    aq_ref[0] = proj[:, o:o + cq]
    o += cq
    akv_ref[0] = proj[:, o:o + ckv]
    o += ckv
    misc_ref[0] = proj[:, o:o + LANE]


def _inproj(x, mod, nw, w_in_p, dims):
    bsz, s, d = x.shape
    cl, cm, cq, ckv = dims
    n_out = w_in_p.shape[1]
    t = min(TOKEN_TILE, s)
    tok = lambda c: pl.BlockSpec((1, t, c), lambda b, i: (b, i, 0))
    return pl.pallas_call(
        functools.partial(_inproj_kernel, dims=dims),
        grid=(bsz, s // t),
        in_specs=[
            tok(d),
            pl.BlockSpec((N_ADA, 1, 1, d), lambda b, i: (0, b, 0, 0)),
            pl.BlockSpec((1, d), lambda b, i: (0, 0)),
            pl.BlockSpec((d, n_out), lambda b, i: (0, 0)),
        ],
        out_specs=[tok(2 * cl), tok(3 * cm), tok(cm), tok(cq), tok(ckv), tok(LANE)],
        out_shape=[
            jax.ShapeDtypeStruct((bsz, s, 2 * cl), F32),
            jax.ShapeDtypeStruct((bsz, s, 3 * cm), BF16),
            jax.ShapeDtypeStruct((bsz, s, cm), F32),
            jax.ShapeDtypeStruct((bsz, s, cq), F32),
            jax.ShapeDtypeStruct((bsz, s, ckv), F32),
            jax.ShapeDtypeStruct((bsz, s, LANE), F32),
        ],
        compiler_params=_params(("parallel", "parallel"), 56),
        name="in_proj",
    )(x, mod, nw, w_in_p)


def _lru_kernel(p_ref, cw_ref, cb_ref, wa_ref, ba_ref, wx_ref, bx_ref, lam_ref, g_ref, out_ref, hc_ref, xt_ref):
    t_len = min(TOKEN_TILE, p_ref.shape[1])

    @pl.when(pl.program_id(1) == 0)
    def _():
        hc_ref[...] = jnp.zeros_like(hc_ref)
        xt_ref[...] = jnp.zeros_like(xt_ref)

    lax.fori_loop(0, p_ref.shape[1] // t_len,
                  functools.partial(_lru_tile, p_ref, cw_ref, cb_ref, wa_ref, ba_ref, wx_ref, bx_ref, lam_ref, g_ref,
                                    out_ref, hc_ref, xt_ref, t_len), 0)


def _lru_tile(p_ref, cw_ref, cb_ref, wa_ref, ba_ref, wx_ref, bx_ref, lam_ref, g_ref, out_ref, hc_ref, xt_ref, t_len,
              ti, carry):
    c = p_ref.shape[2] // 2
    rows = pl.ds(pl.multiple_of(ti * t_len, t_len), t_len)
    p = p_ref[0, rows, :]
    x = p[:, :c]
    y = p[:, c:]
    cw = cw_ref[...]
    xe = jnp.concatenate([xt_ref[...], x], axis=0)
    xc = cb_ref[...] + cw[CONV_WIDTH - 1:CONV_WIDTH] * x
    for j in range(1, CONV_WIDTH):
        xc = xc + cw[CONV_WIDTH - 1 - j:CONV_WIDTH - j] * pltpu.roll(xe, j, axis=0)[8:]
    xt_ref[...] = x[t_len - 8:]

    xb = xc.astype(BF16)
    r = _sigmoid(_dot(xb, wa_ref[...]) + ba_ref[...])
    gi = _sigmoid(_dot(xb, wx_ref[...]) + bx_ref[...])
    nl = -lam_ref[...]
    softplus = jnp.maximum(nl, 0.0) + jnp.log(1.0 + jnp.exp(-jnp.abs(nl)))
    log_a = -LRU_C * r * softplus
    a = jnp.exp(log_a)
    u = jnp.sqrt(1.0 - jnp.exp(2.0 * log_a)) * (gi * xc)

    row = lax.broadcasted_iota(I32, (t_len, c), 0)
    d = 1
    while d < t_len:
        keep = row >= d
        a_s = jnp.where(keep, pltpu.roll(a, d, axis=0), 1.0)
        u_s = jnp.where(keep, pltpu.roll(u, d, axis=0), 0.0)
        u = a * u_s + u
        a = a * a_s
        d *= 2
    h = a * hc_ref[...] + u
    hc_ref[...] = h[t_len - 1:]

    gelu = 0.5 * y * (1.0 + jnp.tanh(math.sqrt(2.0 / math.pi) * (y + 0.044715 * (y * y * y))))
    out_ref[0, rows, :] = _rms(gelu * h, g_ref[...]).astype(BF16)
    return carry


def _lru(p_lru, conv_w, conv_b, wa_bd, ba, wx_bd, bx, lam, g_lru):
    bsz, s, c2 = p_lru.shape
    c = c2 // 2
    t = min(TOKEN_TILE * LRU_STEP_TILES, s)
    row = lambda n: pl.BlockSpec((1, n), lambda b, i: (0, 0))
    full = lambda a: pl.BlockSpec(a.shape, lambda b, i: (0, 0))
    return pl.pallas_call(
        _lru_kernel,
        grid=(bsz, s // t),
        in_specs=[
            pl.BlockSpec((1, t, c2), lambda b, i: (b, i, 0)),
            full(conv_w), row(c), full(wa_bd), row(c), full(wx_bd), row(c), row(c), row(c),
        ],
        out_specs=pl.BlockSpec((1, t, c), lambda b, i: (b, i, 0)),
        out_shape=jax.ShapeDtypeStruct((bsz, s, c), BF16),
        scratch_shapes=[pltpu.VMEM((1, c), F32), pltpu.VMEM((8, c), F32)],
        compiler_params=_params(("arbitrary", "arbitrary"), 32),
        name="rg_lru",
    )(p_lru, conv_w, conv_b, wa_bd, ba, wx_bd, bx, lam, g_lru)


def _mlstm_kernel(qkv_ref, mo_ref, misc_ref, gb_ref, nw_ref, out_ref, ct_ref, m_ref):
    l_len = min(MLSTM_CHUNK, qkv_ref.shape[1])

    @pl.when(pl.program_id(1) == 0)
    def _():
        ct_ref[...] = jnp.zeros_like(ct_ref)
        m_ref[...] = jnp.zeros_like(m_ref)

    lax.fori_loop(0, qkv_ref.shape[1] // l_len,
                  functools.partial(_mlstm_chunk, qkv_ref, mo_ref, misc_ref, gb_ref, nw_ref, out_ref, ct_ref, m_ref,
                                    l_len), 0)


def _mlstm_chunk(qkv_ref, mo_ref, misc_ref, gb_ref, nw_ref, out_ref, ct_ref, m_ref, l_len, ci, carry):
    dh = ct_ref.shape[2]
    nh = ct_ref.shape[0]
    rows = pl.ds(pl.multiple_of(ci * l_len, l_len), l_len)
    g = misc_ref[0, rows, :] + gb_ref[...]
    capped = GATE_SOFTCAP * jnp.tanh(g / GATE_SOFTCAP)
    logf = jnp.minimum(capped, 0.0) - jnp.log(1.0 + jnp.exp(-jnp.abs(capped)))
    row = lax.broadcasted_iota(I32, (l_len, LANE), 0)
    bcum = logf
    d = 1
    while d < l_len:
        bcum = bcum + jnp.where(row >= d, pltpu.roll(bcum, d, axis=0), 0.0)
        d *= 2
    capped_t = capped.T
    bcum_t = bcum.T
    causal = (lax.broadcasted_iota(I32, (l_len, l_len), 0) <= lax.broadcasted_iota(I32, (l_len, l_len), 1))
    ones_rows = jnp.ones((8, l_len), F32)

    for h in range(nh):
        q = qkv_ref[0, rows, h * dh:(h + 1) * dh]
        k = qkv_ref[0, rows, (nh + h) * dh:(nh + h + 1) * dh]
        v = qkv_ref[0, rows, (2 * nh + h) * dh:(2 * nh + h + 1) * dh]
        qs = (q.astype(F32) * (dh ** -0.5)).astype(BF16)
        b_row = bcum_t[nh + h:nh + h + 1, :]
        i_row = capped_t[h:h + 1, :]
        r_col = capped[:, h:h + 1] - bcum[:, nh + h:nh + h + 1]
        m_prev = m_ref[h][:, 0:1]
        ct_old = ct_ref[h]

        dmat = jnp.where(causal, r_col + b_row, -jnp.inf)
        inter = b_row + m_prev
        m_row = jnp.maximum(inter, jnp.max(dmat, axis=0, keepdims=True))
        inter_w = jnp.exp(inter - m_row)
        smat = _dot_nt(k, qs) * jnp.exp(dmat - m_row)
        v_t = v.astype(F32).T
        cq = _dot_nt(ct_old.astype(BF16), qs)
        num = _dot(v_t.astype(BF16), smat.astype(BF16)) + inter_w * cq[:dh]
        den = jnp.sum(smat, axis=0, keepdims=True) + inter_w * cq[dh:dh + 1]
        hh = num / jnp.maximum(jnp.abs(den), jnp.exp(-m_row))

        b_last = b_row[:, l_len - 1:]
        w_row = b_last - b_row + i_row
        m_new = jnp.maximum(b_last + m_prev, jnp.max(w_row, axis=1, keepdims=True))
        decay = jnp.exp(b_last + m_prev - m_new)
        vw = jnp.concatenate([v_t, ones_rows], axis=0) * jnp.exp(w_row - m_new)
        ct_ref[h] = decay * ct_old + _dot(vw.astype(BF16), k)
        m_ref[h] = jnp.broadcast_to(m_new, (1, LANE))

        hn_t = hh * lax.rsqrt(jnp.mean(hh * hh, axis=0, keepdims=True) + NORM_EPS)
        hn = hn_t.T * nw_ref[:, h * dh:(h + 1) * dh]
        gate = _sigmoid(mo_ref[0, rows, h * dh:(h + 1) * dh])
        out_ref[0, rows, h * dh:(h + 1) * dh] = (gate * hn).astype(BF16)
    return carry


def _mlstm(p_qkv, p_mo, p_misc, gate_bias, g_m):
    bsz, s, cm = p_mo.shape
    nh = MLSTM_HEADS
    dh = cm // nh
    l_len = min(MLSTM_CHUNK * MLSTM_STEP_CHUNKS, s)
    tok = lambda c: pl.BlockSpec((1, l_len, c), lambda b, i: (b, i, 0))
    return pl.pallas_call(
        _mlstm_kernel,
        grid=(bsz, s // l_len),
        in_specs=[
            tok(3 * cm), tok(cm), tok(LANE),
            pl.BlockSpec((1, LANE), lambda b, i: (0, 0)),
            pl.BlockSpec((1, cm), lambda b, i: (0, 0)),
        ],
        out_specs=tok(cm),
        out_shape=jax.ShapeDtypeStruct((bsz, s, cm), BF16),
        scratch_shapes=[pltpu.VMEM((nh, dh + 8, dh), F32), pltpu.VMEM((nh, 1, LANE), F32)],
        compiler_params=_params(("arbitrary", "arbitrary"), 32),
        name="mlstm",
    )(p_qkv, p_mo, p_misc, gate_bias, g_m)


def _dsa_kernel(rb_ref, aq_ref, akv_ref, misck_ref, miscq_ref, qnw_ref, kvnw_ref, wq_ref, wqi_ref, wuk_ref, wuv_ref,
                g_ref, out_ref, ckv_s, ik_s, key_s, bits_s, nb_s, acc_s, lg_s, *, topk):
    qi = pl.program_id(1)
    tq = aq_ref.shape[1]
    n_heads = wuk_ref.shape[0]
    dh = wuk_ref.shape[2]
    d_idx = ik_s.shape[1]

    @pl.when(qi == 0)
    def _():
        ckv_s[...] = _rms(akv_ref[0], kvnw_ref[...]).astype(BF16)
        ik_s[...] = misck_ref[0][:, 32:32 + d_idx].astype(BF16)
        bits_s[...] = jnp.zeros_like(bits_s)

    @pl.when((pl.program_id(0) == 0) & (qi == 0))
    def _():
        def build(rb, carry):
            r0 = pl.multiple_of(rb * BIAS_ROWS, BIAS_ROWS)
            rr = lax.broadcasted_iota(I32, (BIAS_ROWS, tq), 0) + r0
            ii = lax.broadcasted_iota(I32, (BIAS_ROWS, tq), 1)
            n = jnp.maximum(ii + 3 * tq - rr, 0)
            log_ratio = jnp.log(jnp.maximum(n, 1).astype(F32) / REL_MAX_EXACT) / math.log(REL_MAX_DIST / REL_MAX_EXACT)
            large = jnp.minimum(REL_MAX_EXACT + (log_ratio * (REL_BUCKETS - REL_MAX_EXACT)).astype(I32),
                                REL_BUCKETS - 1)
            bucket = jnp.where(n < REL_MAX_EXACT, n, large)
            for h in range(n_heads):
                last = rb_ref[REL_BUCKETS - 1, h]
                val = jnp.zeros((BIAS_ROWS, tq), F32)
                for kb in range(REL_BUCKETS - 1):
                    val = jnp.where(bucket == kb, (rb_ref[kb, h] - last) * LOG2E, val)
                nb_s[h, pl.ds(r0, BIAS_ROWS), :] = val
            return carry

        lax.fori_loop(0, nb_s.shape[1] // BIAS_ROWS, build, 0)

    q_lat_t = _rms(aq_ref[0], qnw_ref[...]).T.astype(BF16)
    q_t = _dot(wq_ref[...], q_lat_t).astype(BF16)
    qa_all_t = jnp.concatenate(
        [(_dot(wuk_ref[h], q_t[h * dh:(h + 1) * dh]) * (dh ** -0.5 * LOG2E)).astype(BF16) for h in range(n_heads)],
        axis=1)
    q_idx_t = _dot(wqi_ref[...], q_lat_t) * (d_idx ** -0.5)
    q_idx_all = jnp.concatenate([q_idx_t[h * d_idx:(h + 1) * d_idx] for h in range(IDX_HEADS)], axis=1).astype(BF16)
    w_i = miscq_ref[0].T[8:8 + IDX_HEADS] * (IDX_HEADS ** -0.5)

    t_pos = qi * tq + lax.broadcasted_iota(I32, (SCORE_ROWS, tq), 1)
    row = lax.broadcasted_iota(I32, (SCORE_ROWS, tq), 0)

    def score_group(g):
        for sub in range(WORD_ROWS // SCORE_ROWS):
            r0 = pl.multiple_of(g * WORD_ROWS + sub * SCORE_ROWS, SCORE_ROWS)
            d_all = _dot(ik_s[pl.ds(r0, SCORE_ROWS), :], q_idx_all)
            sc = jnp.zeros((SCORE_ROWS, tq), F32)
            for h in range(IDX_HEADS):
                sc = sc + jnp.maximum(d_all[:, h * tq:(h + 1) * tq], 0.0) * w_i[h:h + 1]
            bits = lax.bitcast_convert_type(sc, I32)
            key = jnp.where(bits < 0, bits ^ INT_MAX, bits)
            key_s[pl.ds(r0, SCORE_ROWS), :] = jnp.where(row + r0 <= t_pos, key, INT_MIN)

    lane_tiles = tq // LANE

    def slice_group(g):
        k0 = pl.multiple_of(g * WORD_ROWS, WORD_ROWS)
        w0 = pl.multiple_of(g * 8, 8)
        for lt in range(lane_tiles):
            cols = slice(lt * LANE, (lt + 1) * LANE)
            w = _bit_transpose32([key_s[pl.ds(k0 + j * 8, 8), cols] ^ INT_MIN for j in range(32)])
            for i in range(32):
                bits_s[31 - i, pl.ds(w0, 8), cols] = w[i]

    def score_and_slice(g, carry):
        slice_group(g - 1)
        score_group(g)
        return carry

    n_word_groups = (qi + 1) * (tq // WORD_ROWS)

    @pl.when((qi & 1) == 0)
    def _():
        key_s[pl.ds(pl.multiple_of((qi + 1) * tq, tq), tq), :] = jnp.full((tq, tq), INT_MIN, I32)

    score_group(0)
    lax.fori_loop(1, n_word_groups, score_and_slice, 0)
    slice_group(n_word_groups - 1)

    n_word_rows = bits_s.shape[1]
    wrow = lax.broadcasted_iota(I32, (n_word_rows, tq), 0)
    first_key = lax.shift_right_logical(wrow, 3) * WORD_ROWS + (wrow & 7)
    t_lane = qi * tq + lax.broadcasted_iota(I32, (n_word_rows, tq), 1)
    n_valid = jnp.clip(lax.shift_right_arithmetic(t_lane - first_key, 3) + 1, 0, 32)
    eq0 = jnp.where(n_valid <= 0, 0, lax.shift_left(jnp.int32(-1), 32 - jnp.maximum(n_valid, 1)))

    def bit_step(i, carry):
        eq, c_gt, ans = carry
        b = 31 - i
        t = eq & bits_s[b]
        cnt = jnp.sum(lax.population_count(t), axis=0, keepdims=True)
        take = (c_gt + cnt) >= topk
        eq = jnp.where(take, t, eq ^ t)
        c_gt = jnp.where(take, c_gt, c_gt + cnt)
        ans = jnp.where(take, ans | lax.shift_left(jnp.int32(1), b), ans)
        return eq, c_gt, ans

    zero_row = jnp.zeros((1, tq), I32)
    _, _, ans = lax.fori_loop(0, 32, bit_step, (eq0, zero_row, zero_row))
    thr = jnp.maximum(ans ^ INT_MIN, INT_MIN + 1)

    c_last = lax.shift_right_logical(qi, 1)
    acc_s[...] = jnp.zeros_like(acc_s)

    def masked_logits(c, slot):
        k0 = pl.multiple_of(c * KV_CHUNK, KV_CHUNK)
        sel = key_s[pl.ds(k0, KV_CHUNK), :] >= thr
        logits_all = _dot(ckv_s[pl.ds(k0, KV_CHUNK), :], qa_all_t)
        b0 = pl.multiple_of(jnp.maximum(2 * c - qi + 3, 0) * tq, tq)
        mx = []
        for h in range(n_heads):
            logits = logits_all[:, h * tq:(h + 1) * tq] + nb_s[h, pl.ds(b0, KV_CHUNK), :]
            logits = jnp.where(sel, logits, NEG_BIG)
            lg_s[slot, h] = logits
            mx.append(jnp.max(logits, axis=0, keepdims=True))
        return tuple(mx)

    def per_query_rows(rows):
        stacked = jnp.concatenate(rows, axis=1)
        return jnp.broadcast_to(stacked, (acc_s.shape[1], stacked.shape[1])).T

    def softmax_update(c, slot, mx, m_old, l_old):
        k0 = pl.multiple_of(c * KV_CHUNK, KV_CHUNK)
        m_out, l_out, alphas, ps = [], [], [], []
        for h in range(n_heads):
            m_new = jnp.maximum(m_old[h], mx[h])
            alpha = jnp.exp2(m_old[h] - m_new)
            p = jnp.exp2(lg_s[slot, h] - m_new)
            l_out.append(alpha * l_old[h] + jnp.sum(p, axis=0, keepdims=True))
            m_out.append(m_new)
            alphas.append(alpha)
            ps.append(p.astype(BF16))
        pv = lax.dot_general(jnp.concatenate(ps, axis=1), ckv_s[pl.ds(k0, KV_CHUNK), :], (((0,), (0,)), ((), ())),
                             preferred_element_type=F32)
        acc_s[...] = per_query_rows(alphas) * acc_s[...] + pv
        return tuple(m_out), tuple(l_out)

    def attend(c, carry):
        mx, m_old, l_old = carry
        m_new, l_new = softmax_update(c, c & 1, mx, m_old, l_old)
        return masked_logits(c + 1, (c + 1) & 1), m_new, l_new

    carry = (masked_logits(0, 0),
             tuple(jnp.full((1, tq), M_INIT, F32) for _ in range(n_heads)),
             tuple(jnp.zeros((1, tq), F32) for _ in range(n_heads)))
    carry = lax.fori_loop(0, c_last, attend, carry)
    _, l_fin = softmax_update(c_last, c_last & 1, *carry)
    o_lat = (acc_s[...] / per_query_rows(l_fin)).astype(BF16)
    o = jnp.concatenate([_dot(o_lat[h * tq:(h + 1) * tq], wuv_ref[h]) for h in range(n_heads)], axis=1)
    out_ref[0] = _rms(o, g_ref[...]).astype(BF16)


def _dsa(rel_bias, p_aq, p_akv, p_misc, qnw, kvnw, wq, wqi, wuk_t, wuv_h, g_a):
    bsz, s, cq = p_aq.shape
    ckv = p_akv.shape[2]
    d_idx = wqi.shape[0] // IDX_HEADS
    topk = min(INDEX_TOPK, s // 4)
    tq = Q_BLOCK
    assert s % KV_CHUNK == 0 and tq % WORD_ROWS == 0 and WORD_ROWS % SCORE_ROWS == 0
    cw = wq.shape[0]
    full2 = lambda a: pl.BlockSpec(a.shape, lambda b, i: (0, 0))
    full3 = lambda a: pl.BlockSpec(a.shape, lambda b, i: (0, 0, 0))
    return pl.pallas_call(
        functools.partial(_dsa_kernel, topk=topk),
        grid=(bsz, s // tq),
        in_specs=[
            pl.BlockSpec(memory_space=pltpu.SMEM),
            pl.BlockSpec((1, tq, cq), lambda b, i: (b, i, 0)),
            pl.BlockSpec((1, s, ckv), lambda b, i: (b, 0, 0)),
            pl.BlockSpec((1, s, LANE), lambda b, i: (b, 0, 0)),
            pl.BlockSpec((1, tq, LANE), lambda b, i: (b, i, 0)),
            full2(qnw), full2(kvnw), full2(wq), full2(wqi), full3(wuk_t), full3(wuv_h), full2(g_a),
        ],
        out_specs=pl.BlockSpec((1, tq, cw), lambda b, i: (b, i, 0)),
        out_shape=jax.ShapeDtypeStruct((bsz, s, cw), BF16),
        scratch_shapes=[
            pltpu.VMEM((s, ckv), BF16),
            pltpu.VMEM((s, d_idx), BF16),
            pltpu.VMEM((s, tq), I32),
            pltpu.VMEM((32, s // 32, tq), I32),
            pltpu.VMEM((ATTN_HEADS, 5 * tq, tq), F32),
            pltpu.VMEM((ATTN_HEADS * tq, ckv), F32),
            pltpu.VMEM((2, ATTN_HEADS, KV_CHUNK, tq), F32),
        ],
        compiler_params=_params(("arbitrary", "arbitrary"), 48),
        name="dsa",
    )(rel_bias, p_aq, p_akv, p_misc, p_misc, qnw, kvnw, wq, wqi, wuk_t, wuv_h, g_a)


def _out_mlp_kernel(x_ref, yl_ref, ym_ref, ya_ref, mod_ref, wo_ref, n2_ref, w1_ref, w2_ref, fw_ref, out_ref,
                    h2_s, acc_s, *, final):
    j = pl.program_id(2)

    @pl.when(j == 0)
    def _():
        cat = jnp.concatenate([yl_ref[0], ym_ref[0], ya_ref[0]], axis=1)
        x1 = x_ref[0] + mod_ref[2, 0] * _dot(cat, wo_ref[...])
        out_ref[0] = x1
        h2_s[...] = (_rms(x1, n2_ref[...]) * (1.0 + mod_ref[4, 0]) + mod_ref[3, 0]).astype(BF16)
        acc_s[...] = jnp.zeros_like(acc_s)

    a = jnp.maximum(_dot(h2_s[...], w1_ref[...]), 0.0)
    acc_s[...] += _dot((a * a).astype(BF16), w2_ref[...])

    @pl.when(j == pl.num_programs(2) - 1)
    def _():
        x2 = out_ref[0] + mod_ref[5, 0] * acc_s[...]
        if final:
            x2 = _rms(x2, fw_ref[...])
        out_ref[0] = x2


def _out_mlp(x, y_lru, y_m, y_a, mod, wo, n2w, w1, w2, fw, final):
    bsz, s, d = x.shape
    dff = w1.shape[1]
    t = min(MLP_TILE, s)
    fc = min(FF_CHUNK, dff)
    tok = lambda c: pl.BlockSpec((1, t, c), lambda b, i, j: (b, i, 0))
    return pl.pallas_call(
        functools.partial(_out_mlp_kernel, final=final),
        grid=(bsz, s // t, dff // fc),
        in_specs=[
            tok(d), tok(y_lru.shape[2]), tok(y_m.shape[2]), tok(y_a.shape[2]),
            pl.BlockSpec((N_ADA, 1, 1, d), lambda b, i, j: (0, b, 0, 0)),
            pl.BlockSpec((d, d), lambda b, i, j: (0, 0)),
            pl.BlockSpec((1, d), lambda b, i, j: (0, 0)),
            pl.BlockSpec((d, fc), lambda b, i, j: (0, j)),
            pl.BlockSpec((fc, d), lambda b, i, j: (j, 0)),
            pl.BlockSpec((1, d), lambda b, i, j: (0, 0)),
        ],
        out_specs=tok(d),
        out_shape=jax.ShapeDtypeStruct((bsz, s, d), F32),
        scratch_shapes=[pltpu.VMEM((t, d), BF16), pltpu.VMEM((t, d), F32)],
        compiler_params=_params(("parallel", "parallel", "arbitrary"), 56),
        name="out_mlp",
    )(x, y_lru, y_m, y_a, mod, wo, n2w, w1, w2, fw)


def _block_diag(w):
    nb, bi, bo = w.shape
    out = jnp.zeros((nb * bi, nb * bo), w.dtype)
    for n in range(nb):
        out = out.at[n * bi:(n + 1) * bi, n * bo:(n + 1) * bo].set(w[n])
    return out


def kernel(x, c, w_in, conv_w, conv_b, lru_wa, lru_ba, lru_wx, lru_bx, lru_lambda, mlstm_bi, mlstm_bf, w_q_up,
           w_qidx_up, w_uk, w_uv, q_lat_norm_w, kv_lat_norm_w, rel_bias, group_norm_w, w_o, w_ada, b_ada, norm1_w,
           norm2_w, w_mlp1, w_mlp2, final_norm_w):
    depth, d, _ = w_in.shape
    bsz = x.shape[0]
    cl = conv_w.shape[2]
    cm = d // 2
    cq = w_q_up.shape[1]
    ckv = w_uk.shape[1]
    d_idx = w_qidx_up.shape[3]
    nh = MLSTM_HEADS
    dims = (cl, cm, cq, ckv)

    mod_all = _ada(c, w_ada, b_ada).reshape(depth, N_ADA, bsz, 1, d)

    o_gate = 2 * cl + 4 * cm
    o_aq = o_gate + 2 * nh
    o_akv = o_aq + cq
    o_ik = o_akv + ckv
    o_iw = o_ik + d_idx
    zeros = lambda n: jnp.zeros((depth, d, n), w_in.dtype)
    w_in_p = jnp.concatenate([
        w_in[:, :, :o_gate], w_in[:, :, o_aq:o_akv], w_in[:, :, o_akv:o_ik],
        w_in[:, :, o_gate:o_aq], w_in[:, :, o_iw:o_iw + IDX_HEADS], zeros(32 - 2 * nh - IDX_HEADS),
        w_in[:, :, o_ik:o_iw], zeros(LANE - 32 - d_idx),
    ], axis=2).astype(BF16)

    gate_bias = jnp.concatenate([mlstm_bi, mlstm_bf, jnp.zeros((depth, LANE - 2 * nh), F32)], axis=1)
    fw = final_norm_w.reshape(1, d)

    for l in range(depth):
        mod = mod_all[l]
        p_lru, p_qkv, p_mo, p_aq, p_akv, p_misc = _inproj(x, mod, norm1_w[l].reshape(1, d), w_in_p[l], dims)
        gw = group_norm_w[l]
        y_lru = _lru(p_lru, conv_w[l], conv_b[l].reshape(1, cl),
                     _block_diag(lru_wa[l]).astype(BF16), lru_ba[l].reshape(1, cl),
                     _block_diag(lru_wx[l]).astype(BF16), lru_bx[l].reshape(1, cl),
                     lru_lambda[l].reshape(1, cl), gw[:cl].reshape(1, cl))
        y_m = _mlstm(p_qkv, p_mo, p_misc, gate_bias[l].reshape(1, LANE), gw[cl:cl + cm].reshape(1, cm))
        wq = w_q_up[l].reshape(cq, -1).T.astype(BF16)
        wqi = w_qidx_up[l].reshape(cq, -1).T.astype(BF16)
        wuk_t = jnp.transpose(w_uk[l], (1, 0, 2)).astype(BF16)
        wuv_h = jnp.transpose(w_uv[l], (1, 0, 2)).astype(BF16)
        y_a = _dsa(rel_bias, p_aq, p_akv, p_misc, q_lat_norm_w[l].reshape(1, cq), kv_lat_norm_w[l].reshape(1, ckv),
                   wq, wqi, wuk_t, wuv_h, gw[cl + cm:].reshape(1, -1))
        x = _out_mlp(x, y_lru, y_m, y_a, mod, w_o[l].astype(BF16), norm2_w[l].reshape(1, d),
                     w_mlp1[l].astype(BF16), w_mlp2[l].astype(BF16), fw, final=(l == depth - 1))
    return x
```

```python
import functools
import math

import jax
import jax.numpy as jnp
from jax import lax
from jax.experimental import pallas as pl
from jax.experimental.pallas import tpu as pltpu

F32 = jnp.float32
BF16 = jnp.bfloat16
I32 = jnp.int32

NORM_EPS = 1e-6
CONV_WIDTH = 4
LRU_C = 8.0
MLSTM_HEADS = 4
GATE_SOFTCAP = 15.0
ATTN_HEADS = 4
IDX_HEADS = 8
INDEX_TOPK = 256
REL_BUCKETS = 32
REL_MAX_EXACT = 16
REL_MAX_DIST = 128
N_ADA = 6

LANE = 128
Q_BLOCK = 256
KV_CHUNK = 2 * Q_BLOCK
BIAS_ROWS = 64
WORD_ROWS = 256
SCORE_ROWS = 128
MLSTM_CHUNK = 256
MLSTM_STEP_CHUNKS = 4
TOKEN_TILE = 512
LRU_STEP_TILES = 4
MLP_TILE = 1024
FF_CHUNK = 1024
INT_MIN = -(2 ** 31)
INT_MAX = 2 ** 31 - 1
NEG_BIG = -1e30
M_INIT = -1e29
LOG2E = math.log2(math.e)
MIB = 1024 * 1024


def _rms(x, w):
    return x * lax.rsqrt(jnp.mean(x * x, axis=-1, keepdims=True) + NORM_EPS) * w


def _sigmoid(x):
    return 1.0 / (1.0 + jnp.exp(-x))


def _dot(a, b):
    return jnp.dot(a, b, preferred_element_type=F32)


def _dot_nt(a, b):
    return lax.dot_general(a, b, (((1,), (1,)), ((), ())), preferred_element_type=F32)


def _bit_transpose32(a):
    a = list(a)
    mask = 0x0000FFFF
    j = 16
    while j:
        k = 0
        while k < 32:
            t = (a[k] ^ lax.shift_right_logical(a[k | j], j)) & mask
            a[k] = a[k] ^ t
            a[k | j] = a[k | j] ^ lax.shift_left(t, j)
            k = ((k | j) + 1) & ~j
        j >>= 1
        mask = (mask ^ (mask << j)) & 0xFFFFFFFF
    return a


def _params(sem, vmem_mib):
    return pltpu.CompilerParams(dimension_semantics=sem, vmem_limit_bytes=vmem_mib * MIB)


def _ada_kernel(c_ref, w_ref, b_ref, o_ref):
    c = c_ref[...]
    act = (c * _sigmoid(c)).astype(BF16)
    o_ref[0, 0] = _dot(act, w_ref[0].astype(BF16)) + b_ref[0, 0]


def _ada(c, w_ada, b_ada):
    depth, d, _ = w_ada.shape
    bsz = c.shape[0]
    b4 = b_ada.reshape(depth, N_ADA, 1, d)
    return pl.pallas_call(
        _ada_kernel,
        grid=(depth, N_ADA),
        in_specs=[
            pl.BlockSpec((bsz, d), lambda l, k: (0, 0)),
            pl.BlockSpec((1, d, d), lambda l, k: (l, 0, k)),
            pl.BlockSpec((1, 1, 1, d), lambda l, k: (l, k, 0, 0)),
        ],
        out_specs=pl.BlockSpec((1, 1, bsz, d), lambda l, k: (l, k, 0, 0)),
        out_shape=jax.ShapeDtypeStruct((depth, N_ADA, bsz, d), F32),
        compiler_params=_params(("arbitrary", "arbitrary"), 32),
        name="ada_mod",
    )(c, w_ada, b4)


def _inproj_kernel(x_ref, mod_ref, nw_ref, w_ref, lru_ref, qkv_ref, mo_ref, aq_ref, akv_ref, misc_ref, *, dims):
    cl, cm, cq, ckv = dims
    x = x_ref[0]
    sh = mod_ref[0, 0]
    sc = mod_ref[1, 0]
    h = _rms(x, nw_ref[...]) * (1.0 + sc) + sh
    proj = _dot(h.astype(BF16), w_ref[...])
    o = 0
    lru_ref[0] = proj[:, o:o + 2 * cl]
    o += 2 * cl
    qkv_ref[0] = proj[:, o:o + 3 * cm].astype(BF16)
    o += 3 * cm
    mo_ref[0] = proj[:, o:o + cm]
    o += cm
    aq_ref[0] = proj[:, o:o + cq]
    o += cq
    akv_ref[0] = proj[:, o:o + ckv]
    o += ckv
    misc_ref[0] = proj[:, o:o + LANE]


def _inproj(x, mod, nw, w_in_p, dims):
    bsz, s, d = x.shape
    cl, cm, cq, ckv = dims
    n_out = w_in_p.shape[1]
    t = min(TOKEN_TILE, s)
    tok = lambda c: pl.BlockSpec((1, t, c), lambda b, i: (b, i, 0))
    return pl.pallas_call(
        functools.partial(_inproj_kernel, dims=dims),
        grid=(bsz, s // t),
        in_specs=[
            tok(d),
            pl.BlockSpec((N_ADA, 1, 1, d), lambda b, i: (0, b, 0, 0)),
            pl.BlockSpec((1, d), lambda b, i: (0, 0)),
            pl.BlockSpec((d, n_out), lambda b, i: (0, 0)),
        ],
        out_specs=[tok(2 * cl), tok(3 * cm), tok(cm), tok(cq), tok(ckv), tok(LANE)],
        out_shape=[
            jax.ShapeDtypeStruct((bsz, s, 2 * cl), F32),
            jax.ShapeDtypeStruct((bsz, s, 3 * cm), BF16),
            jax.ShapeDtypeStruct((bsz, s, cm), F32),
            jax.ShapeDtypeStruct((bsz, s, cq), F32),
            jax.ShapeDtypeStruct((bsz, s, ckv), F32),
            jax.ShapeDtypeStruct((bsz, s, LANE), F32),
        ],
        compiler_params=_params(("parallel", "parallel"), 56),
        name="in_proj",
    )(x, mod, nw, w_in_p)


def _lru_kernel(p_ref, cw_ref, cb_ref, wa_ref, ba_ref, wx_ref, bx_ref, lam_ref, g_ref, out_ref, hc_ref, xt_ref):
    t_len = min(TOKEN_TILE, p_ref.shape[1])

    @pl.when(pl.program_id(1) == 0)
    def _():
        hc_ref[...] = jnp.zeros_like(hc_ref)
        xt_ref[...] = jnp.zeros_like(xt_ref)

    lax.fori_loop(0, p_ref.shape[1] // t_len,
                  functools.partial(_lru_tile, p_ref, cw_ref, cb_ref, wa_ref, ba_ref, wx_ref, bx_ref, lam_ref, g_ref,
                                    out_ref, hc_ref, xt_ref, t_len), 0)


def _lru_tile(p_ref, cw_ref, cb_ref, wa_ref, ba_ref, wx_ref, bx_ref, lam_ref, g_ref, out_ref, hc_ref, xt_ref, t_len,
              ti, carry):
    c = p_ref.shape[2] // 2
    rows = pl.ds(pl.multiple_of(ti * t_len, t_len), t_len)
    p = p_ref[0, rows, :]
    x = p[:, :c]
    y = p[:, c:]
    cw = cw_ref[...]
    xe = jnp.concatenate([xt_ref[...], x], axis=0)
    xc = cb_ref[...] + cw[CONV_WIDTH - 1:CONV_WIDTH] * x
    for j in range(1, CONV_WIDTH):
        xc = xc + cw[CONV_WIDTH - 1 - j:CONV_WIDTH - j] * pltpu.roll(xe, j, axis=0)[8:]
    xt_ref[...] = x[t_len - 8:]

    xb = xc.astype(BF16)
    r = _sigmoid(_dot(xb, wa_ref[...]) + ba_ref[...])
    gi = _sigmoid(_dot(xb, wx_ref[...]) + bx_ref[...])
    nl = -lam_ref[...]
    softplus = jnp.maximum(nl, 0.0) + jnp.log(1.0 + jnp.exp(-jnp.abs(nl)))
    log_a = -LRU_C * r * softplus
    a = jnp.exp(log_a)
    u = jnp.sqrt(1.0 - jnp.exp(2.0 * log_a)) * (gi * xc)

    row = lax.broadcasted_iota(I32, (t_len, c), 0)
    d = 1
    while d < t_len:
        keep = row >= d
        a_s = jnp.where(keep, pltpu.roll(a, d, axis=0), 1.0)
        u_s = jnp.where(keep, pltpu.roll(u, d, axis=0), 0.0)
        u = a * u_s + u
        a = a * a_s
        d *= 2
    h = a * hc_ref[...] + u
    hc_ref[...] = h[t_len - 1:]

    gelu = 0.5 * y * (1.0 + jnp.tanh(math.sqrt(2.0 / math.pi) * (y + 0.044715 * (y * y * y))))
    out_ref[0, rows, :] = _rms(gelu * h, g_ref[...]).astype(BF16)
    return carry


def _lru(p_lru, conv_w, conv_b, wa_bd, ba, wx_bd, bx, lam, g_lru):
    bsz, s, c2 = p_lru.shape
    c = c2 // 2
    t = min(TOKEN_TILE * LRU_STEP_TILES, s)
    row = lambda n: pl.BlockSpec((1, n), lambda b, i: (0, 0))
    full = lambda a: pl.BlockSpec(a.shape, lambda b, i: (0, 0))
    return pl.pallas_call(
        _lru_kernel,
        grid=(bsz, s // t),
        in_specs=[
            pl.BlockSpec((1, t, c2), lambda b, i: (b, i, 0)),
            full(conv_w), row(c), full(wa_bd), row(c), full(wx_bd), row(c), row(c), row(c),
        ],
        out_specs=pl.BlockSpec((1, t, c), lambda b, i: (b, i, 0)),
        out_shape=jax.ShapeDtypeStruct((bsz, s, c), BF16),
        scratch_shapes=[pltpu.VMEM((1, c), F32), pltpu.VMEM((8, c), F32)],
        compiler_params=_params(("arbitrary", "arbitrary"), 32),
        name="rg_lru",
    )(p_lru, conv_w, conv_b, wa_bd, ba, wx_bd, bx, lam, g_lru)


def _mlstm_kernel(qkv_ref, mo_ref, misc_ref, gb_ref, nw_ref, out_ref, ct_ref, m_ref):
    l_len = min(MLSTM_CHUNK, qkv_ref.shape[1])

    @pl.when(pl.program_id(1) == 0)
    def _():
        ct_ref[...] = jnp.zeros_like(ct_ref)
        m_ref[...] = jnp.zeros_like(m_ref)

    lax.fori_loop(0, qkv_ref.shape[1] // l_len,
                  functools.partial(_mlstm_chunk, qkv_ref, mo_ref, misc_ref, gb_ref, nw_ref, out_ref, ct_ref, m_ref,
                                    l_len), 0)


def _mlstm_chunk(qkv_ref, mo_ref, misc_ref, gb_ref, nw_ref, out_ref, ct_ref, m_ref, l_len, ci, carry):
    dh = ct_ref.shape[2]
    nh = ct_ref.shape[0]
    rows = pl.ds(pl.multiple_of(ci * l_len, l_len), l_len)
    g = misc_ref[0, rows, :] + gb_ref[...]
    capped = GATE_SOFTCAP * jnp.tanh(g / GATE_SOFTCAP)
    logf = jnp.minimum(capped, 0.0) - jnp.log(1.0 + jnp.exp(-jnp.abs(capped)))
    row = lax.broadcasted_iota(I32, (l_len, LANE), 0)
    bcum = logf
    d = 1
    while d < l_len:
        bcum = bcum + jnp.where(row >= d, pltpu.roll(bcum, d, axis=0), 0.0)
        d *= 2
    capped_t = capped.T
    bcum_t = bcum.T
    causal = (lax.broadcasted_iota(I32, (l_len, l_len), 0) <= lax.broadcasted_iota(I32, (l_len, l_len), 1))
    ones_rows = jnp.ones((8, l_len), F32)

    for h in range(nh):
        q = qkv_ref[0, rows, h * dh:(h + 1) * dh]
        k = qkv_ref[0, rows, (nh + h) * dh:(nh + h + 1) * dh]
        v = qkv_ref[0, rows, (2 * nh + h) * dh:(2 * nh + h + 1) * dh]
        qs = (q.astype(F32) * (dh ** -0.5)).astype(BF16)
        b_row = bcum_t[nh + h:nh + h + 1, :]
        i_row = capped_t[h:h + 1, :]
        r_col = capped[:, h:h + 1] - bcum[:, nh + h:nh + h + 1]
        m_prev = m_ref[h][:, 0:1]
        ct_old = ct_ref[h]

        dmat = jnp.where(causal, r_col + b_row, -jnp.inf)
        inter = b_row + m_prev
        m_row = jnp.maximum(inter, jnp.max(dmat, axis=0, keepdims=True))
        inter_w = jnp.exp(inter - m_row)
        smat = _dot_nt(k, qs) * jnp.exp(dmat - m_row)
        v_t = v.astype(F32).T
        cq = _dot_nt(ct_old.astype(BF16), qs)
        num = _dot(v_t.astype(BF16), smat.astype(BF16)) + inter_w * cq[:dh]
        den = jnp.sum(smat, axis=0, keepdims=True) + inter_w * cq[dh:dh + 1]
        hh = num / jnp.maximum(jnp.abs(den), jnp.exp(-m_row))

        b_last = b_row[:, l_len - 1:]
        w_row = b_last - b_row + i_row
        m_new = jnp.maximum(b_last + m_prev, jnp.max(w_row, axis=1, keepdims=True))
        decay = jnp.exp(b_last + m_prev - m_new)
        vw = jnp.concatenate([v_t, ones_rows], axis=0) * jnp.exp(w_row - m_new)
        ct_ref[h] = decay * ct_old + _dot(vw.astype(BF16), k)
        m_ref[h] = jnp.broadcast_to(m_new, (1, LANE))

        hn_t = hh * lax.rsqrt(jnp.mean(hh * hh, axis=0, keepdims=True) + NORM_EPS)
        hn = hn_t.T * nw_ref[:, h * dh:(h + 1) * dh]
        gate = _sigmoid(mo_ref[0, rows, h * dh:(h + 1) * dh])
        out_ref[0, rows, h * dh:(h + 1) * dh] = (gate * hn).astype(BF16)
    return carry


def _mlstm(p_qkv, p_mo, p_misc, gate_bias, g_m):
    bsz, s, cm = p_mo.shape
    nh = MLSTM_HEADS
    dh = cm // nh
    l_len = min(MLSTM_CHUNK * MLSTM_STEP_CHUNKS, s)
    tok = lambda c: pl.BlockSpec((1, l_len, c), lambda b, i: (b, i, 0))
    return pl.pallas_call(
        _mlstm_kernel,
        grid=(bsz, s // l_len),
        in_specs=[
            tok(3 * cm), tok(cm), tok(LANE),
            pl.BlockSpec((1, LANE), lambda b, i: (0, 0)),
            pl.BlockSpec((1, cm), lambda b, i: (0, 0)),
        ],
        out_specs=tok(cm),
        out_shape=jax.ShapeDtypeStruct((bsz, s, cm), BF16),
        scratch_shapes=[pltpu.VMEM((nh, dh + 8, dh), F32), pltpu.VMEM((nh, 1, LANE), F32)],
        compiler_params=_params(("arbitrary", "arbitrary"), 32),
        name="mlstm",
    )(p_qkv, p_mo, p_misc, gate_bias, g_m)


def _dsa_kernel(rb_ref, aq_ref, akv_ref, misck_ref, miscq_ref, qnw_ref, kvnw_ref, wq_ref, wqi_ref, wuk_ref, wuv_ref,
                g_ref, out_ref, ckv_s, ik_s, key_s, bits_s, nb_s, acc_s, lg_s, *, topk):
    qi = pl.program_id(1)
    tq = aq_ref.shape[1]
    n_heads = wuk_ref.shape[0]
    dh = wuk_ref.shape[2]
    d_idx = ik_s.shape[1]

    @pl.when(qi == 0)
    def _():
        ckv_s[...] = _rms(akv_ref[0], kvnw_ref[...]).astype(BF16)
        ik_s[...] = misck_ref[0][:, 32:32 + d_idx].astype(BF16)
        bits_s[...] = jnp.zeros_like(bits_s)

    @pl.when((pl.program_id(0) == 0) & (qi == 0))
    def _():
        def build(rb, carry):
            r0 = pl.multiple_of(rb * BIAS_ROWS, BIAS_ROWS)
            rr = lax.broadcasted_iota(I32, (BIAS_ROWS, tq), 0) + r0
            ii = lax.broadcasted_iota(I32, (BIAS_ROWS, tq), 1)
            n = jnp.maximum(ii + 3 * tq - rr, 0)
            log_ratio = jnp.log(jnp.maximum(n, 1).astype(F32) / REL_MAX_EXACT) / math.log(REL_MAX_DIST / REL_MAX_EXACT)
            large = jnp.minimum(REL_MAX_EXACT + (log_ratio * (REL_BUCKETS - REL_MAX_EXACT)).astype(I32),
                                REL_BUCKETS - 1)
            bucket = jnp.where(n < REL_MAX_EXACT, n, large)
            for h in range(n_heads):
                last = rb_ref[REL_BUCKETS - 1, h]
                val = jnp.zeros((BIAS_ROWS, tq), F32)
                for kb in range(REL_BUCKETS - 1):
                    val = jnp.where(bucket == kb, (rb_ref[kb, h] - last) * LOG2E, val)
                nb_s[h, pl.ds(r0, BIAS_ROWS), :] = val
            return carry

        lax.fori_loop(0, nb_s.shape[1] // BIAS_ROWS, build, 0)

    q_lat_t = _rms(aq_ref[0], qnw_ref[...]).T.astype(BF16)
    q_t = _dot(wq_ref[...], q_lat_t).astype(BF16)
    qa_all_t = jnp.concatenate(
        [(_dot(wuk_ref[h], q_t[h * dh:(h + 1) * dh]) * (dh ** -0.5 * LOG2E)).astype(BF16) for h in range(n_heads)],
        axis=1)
    q_idx_t = _dot(wqi_ref[...], q_lat_t) * (d_idx ** -0.5)
    q_idx_all = jnp.concatenate([q_idx_t[h * d_idx:(h + 1) * d_idx] for h in range(IDX_HEADS)], axis=1).astype(BF16)
    w_i = miscq_ref[0].T[8:8 + IDX_HEADS] * (IDX_HEADS ** -0.5)

    t_pos = qi * tq + lax.broadcasted_iota(I32, (SCORE_ROWS, tq), 1)
    row = lax.broadcasted_iota(I32, (SCORE_ROWS, tq), 0)

    def score_group(g):
        for sub in range(WORD_ROWS // SCORE_ROWS):
            r0 = pl.multiple_of(g * WORD_ROWS + sub * SCORE_ROWS, SCORE_ROWS)
            d_all = _dot(ik_s[pl.ds(r0, SCORE_ROWS), :], q_idx_all)
            sc = jnp.zeros((SCORE_ROWS, tq), F32)
            for h in range(IDX_HEADS):
                sc = sc + jnp.maximum(d_all[:, h * tq:(h + 1) * tq], 0.0) * w_i[h:h + 1]
            bits = lax.bitcast_convert_type(sc, I32)
            key = jnp.where(bits < 0, bits ^ INT_MAX, bits)
            key_s[pl.ds(r0, SCORE_ROWS), :] = jnp.where(row + r0 <= t_pos, key, INT_MIN)

    lane_tiles = tq // LANE

    def slice_group(g):
        k0 = pl.multiple_of(g * WORD_ROWS, WORD_ROWS)
        w0 = pl.multiple_of(g * 8, 8)
        for lt in range(lane_tiles):
            cols = slice(lt * LANE, (lt + 1) * LANE)
            w = _bit_transpose32([key_s[pl.ds(k0 + j * 8, 8), cols] ^ INT_MIN for j in range(32)])
            for i in range(32):
                bits_s[31 - i, pl.ds(w0, 8), cols] = w[i]

    def score_and_slice(g, carry):
        slice_group(g - 1)
        score_group(g)
        return carry

    n_word_groups = (qi + 1) * (tq // WORD_ROWS)

    @pl.when((qi & 1) == 0)
    def _():
        key_s[pl.ds(pl.multiple_of((qi + 1) * tq, tq), tq), :] = jnp.full((tq, tq), INT_MIN, I32)

    score_group(0)
    lax.fori_loop(1, n_word_groups, score_and_slice, 0)
    slice_group(n_word_groups - 1)

    n_word_rows = bits_s.shape[1]
    wrow = lax.broadcasted_iota(I32, (n_word_rows, tq), 0)
    first_key = lax.shift_right_logical(wrow, 3) * WORD_ROWS + (wrow & 7)
    t_lane = qi * tq + lax.broadcasted_iota(I32, (n_word_rows, tq), 1)
    n_valid = jnp.clip(lax.shift_right_arithmetic(t_lane - first_key, 3) + 1, 0, 32)
    eq0 = jnp.where(n_valid <= 0, 0, lax.shift_left(jnp.int32(-1), 32 - jnp.maximum(n_valid, 1)))

    def search(rows):
        def bit_step(i, carry):
            eq, c_gt, ans = carry
            b = 31 - i
            t = eq & bits_s[b, :rows]
            cnt = jnp.sum(lax.population_count(t), axis=0, keepdims=True)
            take = (c_gt + cnt) >= topk
            eq = jnp.where(take, t, eq ^ t)
            c_gt = jnp.where(take, c_gt, c_gt + cnt)
            ans = jnp.where(take, ans | lax.shift_left(jnp.int32(1), b), ans)
            return eq, c_gt, ans

        zero_row = jnp.zeros((1, tq), I32)
        return lax.fori_loop(0, 32, bit_step, (eq0[:rows], zero_row, zero_row))[2]

    quarter = pl.cdiv(n_word_rows, 32) * 8
    used_rows = n_word_groups * 8
    ans = lax.cond(
        used_rows <= 2 * quarter,
        lambda: lax.cond(used_rows <= quarter, lambda: search(quarter), lambda: search(2 * quarter)),
        lambda: lax.cond(used_rows <= 3 * quarter, lambda: search(3 * quarter), lambda: search(n_word_rows)))
    thr = jnp.maximum(ans ^ INT_MIN, INT_MIN + 1)

    c_last = lax.shift_right_logical(qi, 1)
    acc_s[...] = jnp.zeros_like(acc_s)

    def masked_logits(c, slot):
        k0 = pl.multiple_of(c * KV_CHUNK, KV_CHUNK)
        sel = key_s[pl.ds(k0, KV_CHUNK), :] >= thr
        logits_all = _dot(ckv_s[pl.ds(k0, KV_CHUNK), :], qa_all_t)
        b0 = pl.multiple_of(jnp.maximum(2 * c - qi + 3, 0) * tq, tq)
        mx = []
        for h in range(n_heads):
            logits = logits_all[:, h * tq:(h + 1) * tq] + nb_s[h, pl.ds(b0, KV_CHUNK), :]
            logits = jnp.where(sel, logits, NEG_BIG)
            lg_s[slot, h] = logits
            mx.append(jnp.max(logits, axis=0, keepdims=True))
        return tuple(mx)

    def per_query_rows(rows):
        stacked = jnp.concatenate(rows, axis=1)
        return jnp.broadcast_to(stacked, (acc_s.shape[1], stacked.shape[1])).T

    def softmax_update(c, slot, mx, m_old, l_old):
        k0 = pl.multiple_of(c * KV_CHUNK, KV_CHUNK)
        m_out, l_out, alphas, ps = [], [], [], []
        for h in range(n_heads):
            m_new = jnp.maximum(m_old[h], mx[h])
            alpha = jnp.exp2(m_old[h] - m_new)
            p = jnp.exp2(lg_s[slot, h] - m_new)
            l_out.append(alpha * l_old[h] + jnp.sum(p, axis=0, keepdims=True))
            m_out.append(m_new)
            alphas.append(alpha)
            ps.append(p.astype(BF16))
        pv = lax.dot_general(jnp.concatenate(ps, axis=1), ckv_s[pl.ds(k0, KV_CHUNK), :], (((0,), (0,)), ((), ())),
                             preferred_element_type=F32)
        acc_s[...] = per_query_rows(alphas) * acc_s[...] + pv
        return tuple(m_out), tuple(l_out)

    def attend(c, carry):
        mx, m_old, l_old = carry
        m_new, l_new = softmax_update(c, c & 1, mx, m_old, l_old)
        return masked_logits(c + 1, (c + 1) & 1), m_new, l_new

    carry = (masked_logits(0, 0),
             tuple(jnp.full((1, tq), M_INIT, F32) for _ in range(n_heads)),
             tuple(jnp.zeros((1, tq), F32) for _ in range(n_heads)))
    carry = lax.fori_loop(0, c_last, attend, carry)
    _, l_fin = softmax_update(c_last, c_last & 1, *carry)
    o_lat = (acc_s[...] / per_query_rows(l_fin)).astype(BF16)
    o = jnp.concatenate([_dot(o_lat[h * tq:(h + 1) * tq], wuv_ref[h]) for h in range(n_heads)], axis=1)
    out_ref[0] = _rms(o, g_ref[...]).astype(BF16)


def _dsa(rel_bias, p_aq, p_akv, p_misc, qnw, kvnw, wq, wqi, wuk_t, wuv_h, g_a):
    bsz, s, cq = p_aq.shape
    ckv = p_akv.shape[2]
    d_idx = wqi.shape[0] // IDX_HEADS
    topk = min(INDEX_TOPK, s // 4)
    tq = Q_BLOCK
    assert s % KV_CHUNK == 0 and tq % WORD_ROWS == 0 and WORD_ROWS % SCORE_ROWS == 0
    cw = wq.shape[0]
    full2 = lambda a: pl.BlockSpec(a.shape, lambda b, i: (0, 0))
    full3 = lambda a: pl.BlockSpec(a.shape, lambda b, i: (0, 0, 0))
    return pl.pallas_call(
        functools.partial(_dsa_kernel, topk=topk),
        grid=(bsz, s // tq),
        in_specs=[
            pl.BlockSpec(memory_space=pltpu.SMEM),
            pl.BlockSpec((1, tq, cq), lambda b, i: (b, i, 0)),
            pl.BlockSpec((1, s, ckv), lambda b, i: (b, 0, 0)),
            pl.BlockSpec((1, s, LANE), lambda b, i: (b, 0, 0)),
            pl.BlockSpec((1, tq, LANE), lambda b, i: (b, i, 0)),
            full2(qnw), full2(kvnw), full2(wq), full2(wqi), full3(wuk_t), full3(wuv_h), full2(g_a),
        ],
        out_specs=pl.BlockSpec((1, tq, cw), lambda b, i: (b, i, 0)),
        out_shape=jax.ShapeDtypeStruct((bsz, s, cw), BF16),
        scratch_shapes=[
            pltpu.VMEM((s, ckv), BF16),
            pltpu.VMEM((s, d_idx), BF16),
            pltpu.VMEM((s, tq), I32),
            pltpu.VMEM((32, s // 32, tq), I32),
            pltpu.VMEM((ATTN_HEADS, 5 * tq, tq), F32),
            pltpu.VMEM((ATTN_HEADS * tq, ckv), F32),
            pltpu.VMEM((2, ATTN_HEADS, KV_CHUNK, tq), F32),
        ],
        compiler_params=_params(("arbitrary", "arbitrary"), 48),
        name="dsa",
    )(rel_bias, p_aq, p_akv, p_misc, p_misc, qnw, kvnw, wq, wqi, wuk_t, wuv_h, g_a)


def _out_mlp_kernel(x_ref, yl_ref, ym_ref, ya_ref, mod_ref, wo_ref, n2_ref, w1_ref, w2_ref, fw_ref, out_ref,
                    h2_s, acc_s, *, final):
    j = pl.program_id(2)

    @pl.when(j == 0)
    def _():
        cat = jnp.concatenate([yl_ref[0], ym_ref[0], ya_ref[0]], axis=1)
        x1 = x_ref[0] + mod_ref[2, 0] * _dot(cat, wo_ref[...])
        out_ref[0] = x1
        h2_s[...] = (_rms(x1, n2_ref[...]) * (1.0 + mod_ref[4, 0]) + mod_ref[3, 0]).astype(BF16)
        acc_s[...] = jnp.zeros_like(acc_s)

    a = jnp.maximum(_dot(h2_s[...], w1_ref[...]), 0.0)
    acc_s[...] += _dot((a * a).astype(BF16), w2_ref[...])

    @pl.when(j == pl.num_programs(2) - 1)
    def _():
        x2 = out_ref[0] + mod_ref[5, 0] * acc_s[...]
        if final:
            x2 = _rms(x2, fw_ref[...])
        out_ref[0] = x2


def _out_mlp(x, y_lru, y_m, y_a, mod, wo, n2w, w1, w2, fw, final):
    bsz, s, d = x.shape
    dff = w1.shape[1]
    t = min(MLP_TILE, s)
    fc = min(FF_CHUNK, dff)
    tok = lambda c: pl.BlockSpec((1, t, c), lambda b, i, j: (b, i, 0))
    return pl.pallas_call(
        functools.partial(_out_mlp_kernel, final=final),
        grid=(bsz, s // t, dff // fc),
        in_specs=[
            tok(d), tok(y_lru.shape[2]), tok(y_m.shape[2]), tok(y_a.shape[2]),
            pl.BlockSpec((N_ADA, 1, 1, d), lambda b, i, j: (0, b, 0, 0)),
            pl.BlockSpec((d, d), lambda b, i, j: (0, 0)),
            pl.BlockSpec((1, d), lambda b, i, j: (0, 0)),
            pl.BlockSpec((d, fc), lambda b, i, j: (0, j)),
            pl.BlockSpec((fc, d), lambda b, i, j: (j, 0)),
            pl.BlockSpec((1, d), lambda b, i, j: (0, 0)),
        ],
        out_specs=tok(d),
        out_shape=jax.ShapeDtypeStruct((bsz, s, d), F32),
        scratch_shapes=[pltpu.VMEM((t, d), BF16), pltpu.VMEM((t, d), F32)],
        compiler_params=_params(("parallel", "parallel", "arbitrary"), 56),
        name="out_mlp",
    )(x, y_lru, y_m, y_a, mod, wo, n2w, w1, w2, fw)


def _block_diag(w):
    nb, bi, bo = w.shape
    out = jnp.zeros((nb * bi, nb * bo), w.dtype)
    for n in range(nb):
        out = out.at[n * bi:(n + 1) * bi, n * bo:(n + 1) * bo].set(w[n])
    return out


def kernel(x, c, w_in, conv_w, conv_b, lru_wa, lru_ba, lru_wx, lru_bx, lru_lambda, mlstm_bi, mlstm_bf, w_q_up,
           w_qidx_up, w_uk, w_uv, q_lat_norm_w, kv_lat_norm_w, rel_bias, group_norm_w, w_o, w_ada, b_ada, norm1_w,
           norm2_w, w_mlp1, w_mlp2, final_norm_w):
    depth, d, _ = w_in.shape
    bsz = x.shape[0]
    cl = conv_w.shape[2]
    cm = d // 2
    cq = w_q_up.shape[1]
    ckv = w_uk.shape[1]
    d_idx = w_qidx_up.shape[3]
    nh = MLSTM_HEADS
    dims = (cl, cm, cq, ckv)

    mod_all = _ada(c, w_ada, b_ada).reshape(depth, N_ADA, bsz, 1, d)

    o_gate = 2 * cl + 4 * cm
    o_aq = o_gate + 2 * nh
    o_akv = o_aq + cq
    o_ik = o_akv + ckv
    o_iw = o_ik + d_idx
    zeros = lambda n: jnp.zeros((depth, d, n), w_in.dtype)
    w_in_p = jnp.concatenate([
        w_in[:, :, :o_gate], w_in[:, :, o_aq:o_akv], w_in[:, :, o_akv:o_ik],
        w_in[:, :, o_gate:o_aq], w_in[:, :, o_iw:o_iw + IDX_HEADS], zeros(32 - 2 * nh - IDX_HEADS),
        w_in[:, :, o_ik:o_iw], zeros(LANE - 32 - d_idx),
    ], axis=2).astype(BF16)

    gate_bias = jnp.concatenate([mlstm_bi, mlstm_bf, jnp.zeros((depth, LANE - 2 * nh), F32)], axis=1)
    fw = final_norm_w.reshape(1, d)

    for l in range(depth):
        mod = mod_all[l]
        p_lru, p_qkv, p_mo, p_aq, p_akv, p_misc = _inproj(x, mod, norm1_w[l].reshape(1, d), w_in_p[l], dims)
        gw = group_norm_w[l]
        y_lru = _lru(p_lru, conv_w[l], conv_b[l].reshape(1, cl),
                     _block_diag(lru_wa[l]).astype(BF16), lru_ba[l].reshape(1, cl),
                     _block_diag(lru_wx[l]).astype(BF16), lru_bx[l].reshape(1, cl),
                     lru_lambda[l].reshape(1, cl), gw[:cl].reshape(1, cl))
        y_m = _mlstm(p_qkv, p_mo, p_misc, gate_bias[l].reshape(1, LANE), gw[cl:cl + cm].reshape(1, cm))
        wq = w_q_up[l].reshape(cq, -1).T.astype(BF16)
        wqi = w_qidx_up[l].reshape(cq, -1).T.astype(BF16)
        wuk_t = jnp.transpose(w_uk[l], (1, 0, 2)).astype(BF16)
        wuv_h = jnp.transpose(w_uv[l], (1, 0, 2)).astype(BF16)
        y_a = _dsa(rel_bias, p_aq, p_akv, p_misc, q_lat_norm_w[l].reshape(1, cq), kv_lat_norm_w[l].reshape(1, ckv),
                   wq, wqi, wuk_t, wuv_h, gw[cl + cm:].reshape(1, -1))
        x = _out_mlp(x, y_lru, y_m, y_a, mod, w_o[l].astype(BF16), norm2_w[l].reshape(1, d),
                     w_mlp1[l].astype(BF16), w_mlp2[l].astype(BF16), fw, final=(l == depth - 1))
    return x
```

```python
import functools
import math

import jax
import jax.numpy as jnp
from jax import lax
from jax.experimental import pallas as pl
from jax.experimental.pallas import tpu as pltpu

F32 = jnp.float32
BF16 = jnp.bfloat16
I32 = jnp.int32

NORM_EPS = 1e-6
CONV_WIDTH = 4
LRU_C = 8.0
MLSTM_HEADS = 4
GATE_SOFTCAP = 15.0
ATTN_HEADS = 4
IDX_HEADS = 8
INDEX_TOPK = 256
REL_BUCKETS = 32
REL_MAX_EXACT = 16
REL_MAX_DIST = 128
N_ADA = 6

LANE = 128
Q_BLOCK = 256
KV_CHUNK = 2 * Q_BLOCK
BIAS_ROWS = 64
WORD_ROWS = 256
SCORE_ROWS = 128
PROJ_COLS = 1024
MLSTM_CHUNK = 256
MLSTM_STEP_CHUNKS = 4
TOKEN_TILE = 512
LRU_STEP_TILES = 4
MLP_TILE = 1024
FF_CHUNK = 1024
INT_MIN = -(2 ** 31)
INT_MAX = 2 ** 31 - 1
NEG_BIG = -1e30
M_INIT = -1e29
LOG2E = math.log2(math.e)
MIB = 1024 * 1024


def _rms(x, w):
    return x * lax.rsqrt(jnp.mean(x * x, axis=-1, keepdims=True) + NORM_EPS) * w


def _sigmoid(x):
    return 1.0 / (1.0 + jnp.exp(-x))


def _dot(a, b):
    return jnp.dot(a, b, preferred_element_type=F32)


def _dot_nt(a, b):
    return lax.dot_general(a, b, (((1,), (1,)), ((), ())), preferred_element_type=F32)


def _bit_transpose32(a):
    a = list(a)
    mask = 0x0000FFFF
    j = 16
    while j:
        k = 0
        while k < 32:
            t = (a[k] ^ lax.shift_right_logical(a[k | j], j)) & mask
            a[k] = a[k] ^ t
            a[k | j] = a[k | j] ^ lax.shift_left(t, j)
            k = ((k | j) + 1) & ~j
        j >>= 1
        mask = (mask ^ (mask << j)) & 0xFFFFFFFF
    return a


def _params(sem, vmem_mib):
    return pltpu.CompilerParams(dimension_semantics=sem, vmem_limit_bytes=vmem_mib * MIB)


def _ada_kernel(c_ref, w_ref, b_ref, o_ref):
    c = c_ref[...]
    act = (c * _sigmoid(c)).astype(BF16)
    o_ref[0, 0] = _dot(act, w_ref[0].astype(BF16)) + b_ref[0, 0]


def _ada(c, w_ada, b_ada):
    depth, d, _ = w_ada.shape
    bsz = c.shape[0]
    b4 = b_ada.reshape(depth, N_ADA, 1, d)
    return pl.pallas_call(
        _ada_kernel,
        grid=(depth, N_ADA),
        in_specs=[
            pl.BlockSpec((bsz, d), lambda l, k: (0, 0)),
            pl.BlockSpec((1, d, d), lambda l, k: (l, 0, k)),
            pl.BlockSpec((1, 1, 1, d), lambda l, k: (l, k, 0, 0)),
        ],
        out_specs=pl.BlockSpec((1, 1, bsz, d), lambda l, k: (l, k, 0, 0)),
        out_shape=jax.ShapeDtypeStruct((depth, N_ADA, bsz, d), F32),
        compiler_params=_params(("arbitrary", "arbitrary"), 32),
        name="ada_mod",
    )(c, w_ada, b4)


def _inproj_kernel(x_ref, mod_ref, nw_ref, w_ref, lru_ref, qkv_ref, mo_ref, aq_ref, akv_ref, misc_ref, *, dims):
    cl, cm, cq, ckv = dims
    x = x_ref[0]
    sh = mod_ref[0, 0]
    sc = mod_ref[1, 0]
    h = _rms(x, nw_ref[...]) * (1.0 + sc) + sh
    proj = _dot(h.astype(BF16), w_ref[...])
    o = 0
    lru_ref[0] = proj[:, o:o + 2 * cl]
    o += 2 * cl
    qkv_ref[0] = proj[:, o:o + 3 * cm].astype(BF16)
    o += 3 * cm
    mo_ref[0] = proj[:, o:o + cm]
    o += cm
    aq_ref[0] = proj[:, o:o + cq]
    o += cq
    akv_ref[0] = proj[:, o:o + ckv]
    o += ckv
    misc_ref[0] = proj[:, o:o + LANE]


def _inproj(x, mod, nw, w_in_p, dims):
    bsz, s, d = x.shape
    cl, cm, cq, ckv = dims
    n_out = w_in_p.shape[1]
    t = min(TOKEN_TILE, s)
    tok = lambda c: pl.BlockSpec((1, t, c), lambda b, i: (b, i, 0))
    return pl.pallas_call(
        functools.partial(_inproj_kernel, dims=dims),
        grid=(bsz, s // t),
        in_specs=[
            tok(d),
            pl.BlockSpec((N_ADA, 1, 1, d), lambda b, i: (0, b, 0, 0)),
            pl.BlockSpec((1, d), lambda b, i: (0, 0)),
            pl.BlockSpec((d, n_out), lambda b, i: (0, 0)),
        ],
        out_specs=[tok(2 * cl), tok(3 * cm), tok(cm), tok(cq), tok(ckv), tok(LANE)],
        out_shape=[
            jax.ShapeDtypeStruct((bsz, s, 2 * cl), F32),
            jax.ShapeDtypeStruct((bsz, s, 3 * cm), BF16),
            jax.ShapeDtypeStruct((bsz, s, cm), F32),
            jax.ShapeDtypeStruct((bsz, s, cq), F32),
            jax.ShapeDtypeStruct((bsz, s, ckv), F32),
            jax.ShapeDtypeStruct((bsz, s, LANE), F32),
        ],
        compiler_params=_params(("parallel", "parallel"), 56),
        name="in_proj",
    )(x, mod, nw, w_in_p)


def _lru_kernel(p_ref, cw_ref, cb_ref, wa_ref, ba_ref, wx_ref, bx_ref, lam_ref, g_ref, out_ref, hc_ref, xt_ref):
    t_len = min(TOKEN_TILE, p_ref.shape[1])

    @pl.when(pl.program_id(1) == 0)
    def _():
        hc_ref[...] = jnp.zeros_like(hc_ref)
        xt_ref[...] = jnp.zeros_like(xt_ref)

    lax.fori_loop(0, p_ref.shape[1] // t_len,
                  functools.partial(_lru_tile, p_ref, cw_ref, cb_ref, wa_ref, ba_ref, wx_ref, bx_ref, lam_ref, g_ref,
                                    out_ref, hc_ref, xt_ref, t_len), 0)


def _lru_tile(p_ref, cw_ref, cb_ref, wa_ref, ba_ref, wx_ref, bx_ref, lam_ref, g_ref, out_ref, hc_ref, xt_ref, t_len,
              ti, carry):
    c = p_ref.shape[2] // 2
    rows = pl.ds(pl.multiple_of(ti * t_len, t_len), t_len)
    p = p_ref[0, rows, :]
    x = p[:, :c]
    y = p[:, c:]
    cw = cw_ref[...]
    xe = jnp.concatenate([xt_ref[...], x], axis=0)
    xc = cb_ref[...] + cw[CONV_WIDTH - 1:CONV_WIDTH] * x
    for j in range(1, CONV_WIDTH):
        xc = xc + cw[CONV_WIDTH - 1 - j:CONV_WIDTH - j] * pltpu.roll(xe, j, axis=0)[8:]
    xt_ref[...] = x[t_len - 8:]

    xb = xc.astype(BF16)
    r = _sigmoid(_dot(xb, wa_ref[...]) + ba_ref[...])
    gi = _sigmoid(_dot(xb, wx_ref[...]) + bx_ref[...])
    nl = -lam_ref[...]
    softplus = jnp.maximum(nl, 0.0) + jnp.log(1.0 + jnp.exp(-jnp.abs(nl)))
    log_a = -LRU_C * r * softplus
    a = jnp.exp(log_a)
    u = jnp.sqrt(1.0 - jnp.exp(2.0 * log_a)) * (gi * xc)

    row = lax.broadcasted_iota(I32, (t_len, c), 0)
    d = 1
    while d < t_len:
        keep = row >= d
        a_s = jnp.where(keep, pltpu.roll(a, d, axis=0), 1.0)
        u_s = jnp.where(keep, pltpu.roll(u, d, axis=0), 0.0)
        u = a * u_s + u
        a = a * a_s
        d *= 2
    h = a * hc_ref[...] + u
    hc_ref[...] = h[t_len - 1:]

    gelu = 0.5 * y * (1.0 + jnp.tanh(math.sqrt(2.0 / math.pi) * (y + 0.044715 * (y * y * y))))
    out_ref[0, rows, :] = _rms(gelu * h, g_ref[...]).astype(BF16)
    return carry


def _lru(p_lru, conv_w, conv_b, wa_bd, ba, wx_bd, bx, lam, g_lru):
    bsz, s, c2 = p_lru.shape
    c = c2 // 2
    t = min(TOKEN_TILE * LRU_STEP_TILES, s)
    row = lambda n: pl.BlockSpec((1, n), lambda b, i: (0, 0))
    full = lambda a: pl.BlockSpec(a.shape, lambda b, i: (0, 0))
    return pl.pallas_call(
        _lru_kernel,
        grid=(bsz, s // t),
        in_specs=[
            pl.BlockSpec((1, t, c2), lambda b, i: (b, i, 0)),
            full(conv_w), row(c), full(wa_bd), row(c), full(wx_bd), row(c), row(c), row(c),
        ],
        out_specs=pl.BlockSpec((1, t, c), lambda b, i: (b, i, 0)),
        out_shape=jax.ShapeDtypeStruct((bsz, s, c), BF16),
        scratch_shapes=[pltpu.VMEM((1, c), F32), pltpu.VMEM((8, c), F32)],
        compiler_params=_params(("arbitrary", "arbitrary"), 32),
        name="rg_lru",
    )(p_lru, conv_w, conv_b, wa_bd, ba, wx_bd, bx, lam, g_lru)


def _mlstm_kernel(qkv_ref, mo_ref, misc_ref, gb_ref, nw_ref, out_ref, ct_ref, m_ref):
    l_len = min(MLSTM_CHUNK, qkv_ref.shape[1])

    @pl.when(pl.program_id(1) == 0)
    def _():
        ct_ref[...] = jnp.zeros_like(ct_ref)
        m_ref[...] = jnp.zeros_like(m_ref)

    lax.fori_loop(0, qkv_ref.shape[1] // l_len,
                  functools.partial(_mlstm_chunk, qkv_ref, mo_ref, misc_ref, gb_ref, nw_ref, out_ref, ct_ref, m_ref,
                                    l_len), 0)


def _mlstm_chunk(qkv_ref, mo_ref, misc_ref, gb_ref, nw_ref, out_ref, ct_ref, m_ref, l_len, ci, carry):
    dh = ct_ref.shape[2]
    nh = ct_ref.shape[0]
    rows = pl.ds(pl.multiple_of(ci * l_len, l_len), l_len)
    g = misc_ref[0, rows, :] + gb_ref[...]
    capped = GATE_SOFTCAP * jnp.tanh(g / GATE_SOFTCAP)
    logf = jnp.minimum(capped, 0.0) - jnp.log(1.0 + jnp.exp(-jnp.abs(capped)))
    row = lax.broadcasted_iota(I32, (l_len, LANE), 0)
    bcum = logf
    d = 1
    while d < l_len:
        bcum = bcum + jnp.where(row >= d, pltpu.roll(bcum, d, axis=0), 0.0)
        d *= 2
    capped_t = capped.T
    bcum_t = bcum.T
    causal = (lax.broadcasted_iota(I32, (l_len, l_len), 0) <= lax.broadcasted_iota(I32, (l_len, l_len), 1))
    ones_rows = jnp.ones((8, l_len), F32)

    for h in range(nh):
        q = qkv_ref[0, rows, h * dh:(h + 1) * dh]
        k = qkv_ref[0, rows, (nh + h) * dh:(nh + h + 1) * dh]
        v = qkv_ref[0, rows, (2 * nh + h) * dh:(2 * nh + h + 1) * dh]
        qs = (q.astype(F32) * (dh ** -0.5)).astype(BF16)
        b_row = bcum_t[nh + h:nh + h + 1, :]
        i_row = capped_t[h:h + 1, :]
        r_col = capped[:, h:h + 1] - bcum[:, nh + h:nh + h + 1]
        m_prev = m_ref[h][:, 0:1]
        ct_old = ct_ref[h]

        dmat = jnp.where(causal, r_col + b_row, -jnp.inf)
        inter = b_row + m_prev
        m_row = jnp.maximum(inter, jnp.max(dmat, axis=0, keepdims=True))
        inter_w = jnp.exp(inter - m_row)
        smat = _dot_nt(k, qs) * jnp.exp(dmat - m_row)
        v_t = v.astype(F32).T
        cq = _dot_nt(ct_old.astype(BF16), qs)
        num = _dot(v_t.astype(BF16), smat.astype(BF16)) + inter_w * cq[:dh]
        den = jnp.sum(smat, axis=0, keepdims=True) + inter_w * cq[dh:dh + 1]
        hh = num / jnp.maximum(jnp.abs(den), jnp.exp(-m_row))

        b_last = b_row[:, l_len - 1:]
        w_row = b_last - b_row + i_row
        m_new = jnp.maximum(b_last + m_prev, jnp.max(w_row, axis=1, keepdims=True))
        decay = jnp.exp(b_last + m_prev - m_new)
        vw = jnp.concatenate([v_t, ones_rows], axis=0) * jnp.exp(w_row - m_new)
        ct_ref[h] = decay * ct_old + _dot(vw.astype(BF16), k)
        m_ref[h] = jnp.broadcast_to(m_new, (1, LANE))

        hn_t = hh * lax.rsqrt(jnp.mean(hh * hh, axis=0, keepdims=True) + NORM_EPS)
        hn = hn_t.T * nw_ref[:, h * dh:(h + 1) * dh]
        gate = _sigmoid(mo_ref[0, rows, h * dh:(h + 1) * dh])
        out_ref[0, rows, h * dh:(h + 1) * dh] = (gate * hn).astype(BF16)
    return carry


def _mlstm(p_qkv, p_mo, p_misc, gate_bias, g_m):
    bsz, s, cm = p_mo.shape
    nh = MLSTM_HEADS
    dh = cm // nh
    l_len = min(MLSTM_CHUNK * MLSTM_STEP_CHUNKS, s)
    tok = lambda c: pl.BlockSpec((1, l_len, c), lambda b, i: (b, i, 0))
    return pl.pallas_call(
        _mlstm_kernel,
        grid=(bsz, s // l_len),
        in_specs=[
            tok(3 * cm), tok(cm), tok(LANE),
            pl.BlockSpec((1, LANE), lambda b, i: (0, 0)),
            pl.BlockSpec((1, cm), lambda b, i: (0, 0)),
        ],
        out_specs=tok(cm),
        out_shape=jax.ShapeDtypeStruct((bsz, s, cm), BF16),
        scratch_shapes=[pltpu.VMEM((nh, dh + 8, dh), F32), pltpu.VMEM((nh, 1, LANE), F32)],
        compiler_params=_params(("arbitrary", "arbitrary"), 32),
        name="mlstm",
    )(p_qkv, p_mo, p_misc, gate_bias, g_m)


def _dsa_kernel(rb_ref, aq_ref, akv_ref, misc_ref, qnw_ref, kvnw_ref, wq_ref, wqi_ref, wuk_ref, wuv_ref,
                g_ref, out_ref, ckv_s, ik_s, qa_s, qidx_s, wi_s, key_s, bits_s, nb_s, acc_s, lg_s, *, topk):
    qi = pl.program_id(1)
    tq = out_ref.shape[1]
    n_heads = wuk_ref.shape[0]
    dh = wuk_ref.shape[2]
    d_idx = ik_s.shape[1]

    @pl.when(qi == 0)
    def _():
        ckv_s[...] = _rms(akv_ref[0], kvnw_ref[...]).astype(BF16)
        ik_s[...] = misc_ref[0][:, 32:32 + d_idx].astype(BF16)
        bits_s[...] = jnp.zeros_like(bits_s)

        slab = min(PROJ_COLS, aq_ref.shape[1])

        def project(ci, carry):
            rows = pl.ds(pl.multiple_of(ci * slab, slab), slab)
            q_lat_t = _rms(aq_ref[0, rows, :], qnw_ref[...]).T.astype(BF16)
            q_t = _dot(wq_ref[...], q_lat_t).astype(BF16)
            for h in range(n_heads):
                qa_s[h, :, rows] = (_dot(wuk_ref[h], q_t[h * dh:(h + 1) * dh]) * (dh ** -0.5 * LOG2E)).astype(BF16)
            qidx_s[:, rows] = (_dot(wqi_ref[...], q_lat_t) * (d_idx ** -0.5)).astype(BF16)
            wi_s[:, rows] = misc_ref[0, rows, :].T[8:8 + IDX_HEADS] * (IDX_HEADS ** -0.5)
            return carry

        lax.fori_loop(0, aq_ref.shape[1] // slab, project, 0)

    @pl.when((pl.program_id(0) == 0) & (qi == 0))
    def _():
        def build(rb, carry):
            r0 = pl.multiple_of(rb * BIAS_ROWS, BIAS_ROWS)
            rr = lax.broadcasted_iota(I32, (BIAS_ROWS, tq), 0) + r0
            ii = lax.broadcasted_iota(I32, (BIAS_ROWS, tq), 1)
            n = jnp.maximum(ii + 3 * tq - rr, 0)
            log_ratio = jnp.log(jnp.maximum(n, 1).astype(F32) / REL_MAX_EXACT) / math.log(REL_MAX_DIST / REL_MAX_EXACT)
            large = jnp.minimum(REL_MAX_EXACT + (log_ratio * (REL_BUCKETS - REL_MAX_EXACT)).astype(I32),
                                REL_BUCKETS - 1)
            bucket = jnp.where(n < REL_MAX_EXACT, n, large)
            for h in range(n_heads):
                last = rb_ref[REL_BUCKETS - 1, h]
                val = jnp.zeros((BIAS_ROWS, tq), F32)
                for kb in range(REL_BUCKETS - 1):
                    val = jnp.where(bucket == kb, (rb_ref[kb, h] - last) * LOG2E, val)
                nb_s[h, pl.ds(r0, BIAS_ROWS), :] = val
            return carry

        lax.fori_loop(0, nb_s.shape[1] // BIAS_ROWS, build, 0)

    cols = pl.ds(pl.multiple_of(qi * tq, tq), tq)
    qa_all_t = jnp.concatenate([qa_s[h, :, cols] for h in range(n_heads)], axis=1)
    q_idx_all = jnp.concatenate([qidx_s[h * d_idx:(h + 1) * d_idx, cols] for h in range(IDX_HEADS)], axis=1)
    w_i = wi_s[:, cols]

    t_pos = qi * tq + lax.broadcasted_iota(I32, (SCORE_ROWS, tq), 1)
    row = lax.broadcasted_iota(I32, (SCORE_ROWS, tq), 0)

    def score_group(g):
        for sub in range(WORD_ROWS // SCORE_ROWS):
            r0 = pl.multiple_of(g * WORD_ROWS + sub * SCORE_ROWS, SCORE_ROWS)
            d_all = _dot(ik_s[pl.ds(r0, SCORE_ROWS), :], q_idx_all)
            sc = jnp.zeros((SCORE_ROWS, tq), F32)
            for h in range(IDX_HEADS):
                sc = sc + jnp.maximum(d_all[:, h * tq:(h + 1) * tq], 0.0) * w_i[h:h + 1]
            bits = lax.bitcast_convert_type(sc, I32)
            key = jnp.where(bits < 0, bits ^ INT_MAX, bits)
            key_s[pl.ds(r0, SCORE_ROWS), :] = jnp.where(row + r0 <= t_pos, key, INT_MIN)

    lane_tiles = tq // LANE

    def slice_group(g):
        k0 = pl.multiple_of(g * WORD_ROWS, WORD_ROWS)
        w0 = pl.multiple_of(g * 8, 8)
        for lt in range(lane_tiles):
            cols = slice(lt * LANE, (lt + 1) * LANE)
            w = _bit_transpose32([key_s[pl.ds(k0 + j * 8, 8), cols] ^ INT_MIN for j in range(32)])
            for i in range(32):
                bits_s[31 - i, pl.ds(w0, 8), cols] = w[i]

    def score_and_slice(g, carry):
        slice_group(g - 1)
        score_group(g)
        return carry

    n_word_groups = (qi + 1) * (tq // WORD_ROWS)

    @pl.when((qi & 1) == 0)
    def _():
        key_s[pl.ds(pl.multiple_of((qi + 1) * tq, tq), tq), :] = jnp.full((tq, tq), INT_MIN, I32)

    score_group(0)
    lax.fori_loop(1, n_word_groups, score_and_slice, 0)
    slice_group(n_word_groups - 1)

    n_word_rows = bits_s.shape[1]
    wrow = lax.broadcasted_iota(I32, (n_word_rows, tq), 0)
    first_key = lax.shift_right_logical(wrow, 3) * WORD_ROWS + (wrow & 7)
    t_lane = qi * tq + lax.broadcasted_iota(I32, (n_word_rows, tq), 1)
    n_valid = jnp.clip(lax.shift_right_arithmetic(t_lane - first_key, 3) + 1, 0, 32)
    eq0 = jnp.where(n_valid <= 0, 0, lax.shift_left(jnp.int32(-1), 32 - jnp.maximum(n_valid, 1)))

    def search(rows):
        def bit_step(i, carry):
            eq, c_gt, ans = carry
            b = 31 - i
            t = eq & bits_s[b, :rows]
            cnt = jnp.sum(lax.population_count(t), axis=0, keepdims=True)
            take = (c_gt + cnt) >= topk
            eq = jnp.where(take, t, eq ^ t)
            c_gt = jnp.where(take, c_gt, c_gt + cnt)
            ans = jnp.where(take, ans | lax.shift_left(jnp.int32(1), b), ans)
            return eq, c_gt, ans

        zero_row = jnp.zeros((1, tq), I32)
        return lax.fori_loop(0, 32, bit_step, (eq0[:rows], zero_row, zero_row))[2]

    quarter = pl.cdiv(n_word_rows, 32) * 8
    used_rows = n_word_groups * 8
    ans = lax.cond(
        used_rows <= 2 * quarter,
        lambda: lax.cond(used_rows <= quarter, lambda: search(quarter), lambda: search(2 * quarter)),
        lambda: lax.cond(used_rows <= 3 * quarter, lambda: search(3 * quarter), lambda: search(n_word_rows)))
    thr = jnp.maximum(ans ^ INT_MIN, INT_MIN + 1)

    c_last = lax.shift_right_logical(qi, 1)
    acc_s[...] = jnp.zeros_like(acc_s)

    def masked_logits(c, slot):
        k0 = pl.multiple_of(c * KV_CHUNK, KV_CHUNK)
        sel = key_s[pl.ds(k0, KV_CHUNK), :] >= thr
        logits_all = _dot(ckv_s[pl.ds(k0, KV_CHUNK), :], qa_all_t)
        b0 = pl.multiple_of(jnp.maximum(2 * c - qi + 3, 0) * tq, tq)
        mx = []
        for h in range(n_heads):
            logits = logits_all[:, h * tq:(h + 1) * tq] + nb_s[h, pl.ds(b0, KV_CHUNK), :]
            logits = jnp.where(sel, logits, NEG_BIG)
            lg_s[slot, h] = logits
            mx.append(jnp.max(logits, axis=0, keepdims=True))
        return tuple(mx)

    def per_query_rows(rows):
        stacked = jnp.concatenate(rows, axis=1)
        return jnp.broadcast_to(stacked, (acc_s.shape[1], stacked.shape[1])).T

    def softmax_update(c, slot, mx, m_old, l_old):
        k0 = pl.multiple_of(c * KV_CHUNK, KV_CHUNK)
        m_out, l_out, alphas, ps = [], [], [], []
        for h in range(n_heads):
            m_new = jnp.maximum(m_old[h], mx[h])
            alpha = jnp.exp2(m_old[h] - m_new)
            p = jnp.exp2(lg_s[slot, h] - m_new)
            l_out.append(alpha * l_old[h] + jnp.sum(p, axis=0, keepdims=True))
            m_out.append(m_new)
            alphas.append(alpha)
            ps.append(p.astype(BF16))
        pv = lax.dot_general(jnp.concatenate(ps, axis=1), ckv_s[pl.ds(k0, KV_CHUNK), :], (((0,), (0,)), ((), ())),
                             preferred_element_type=F32)
        acc_s[...] = per_query_rows(alphas) * acc_s[...] + pv
        return tuple(m_out), tuple(l_out)

    def attend(c, carry):
        mx, m_old, l_old = carry
        m_new, l_new = softmax_update(c, c & 1, mx, m_old, l_old)
        return masked_logits(c + 1, (c + 1) & 1), m_new, l_new

    carry = (masked_logits(0, 0),
             tuple(jnp.full((1, tq), M_INIT, F32) for _ in range(n_heads)),
             tuple(jnp.zeros((1, tq), F32) for _ in range(n_heads)))
    carry = lax.fori_loop(0, c_last, attend, carry)
    _, l_fin = softmax_update(c_last, c_last & 1, *carry)
    o_lat = (acc_s[...] / per_query_rows(l_fin)).astype(BF16)
    o = jnp.concatenate([_dot(o_lat[h * tq:(h + 1) * tq], wuv_ref[h]) for h in range(n_heads)], axis=1)
    out_ref[0] = _rms(o, g_ref[...]).astype(BF16)


def _dsa(rel_bias, p_aq, p_akv, p_misc, qnw, kvnw, wq, wqi, wuk_t, wuv_h, g_a):
    bsz, s, cq = p_aq.shape
    ckv = p_akv.shape[2]
    d_idx = wqi.shape[0] // IDX_HEADS
    topk = min(INDEX_TOPK, s // 4)
    tq = Q_BLOCK
    assert s % KV_CHUNK == 0 and tq % WORD_ROWS == 0 and WORD_ROWS % SCORE_ROWS == 0
    cw = wq.shape[0]
    full2 = lambda a: pl.BlockSpec(a.shape, lambda b, i: (0, 0))
    full3 = lambda a: pl.BlockSpec(a.shape, lambda b, i: (0, 0, 0))
    return pl.pallas_call(
        functools.partial(_dsa_kernel, topk=topk),
        grid=(bsz, s // tq),
        in_specs=[
            pl.BlockSpec(memory_space=pltpu.SMEM),
            pl.BlockSpec((1, s, cq), lambda b, i: (b, 0, 0)),
            pl.BlockSpec((1, s, ckv), lambda b, i: (b, 0, 0)),
            pl.BlockSpec((1, s, LANE), lambda b, i: (b, 0, 0)),
            full2(qnw), full2(kvnw), full2(wq), full2(wqi), full3(wuk_t), full3(wuv_h), full2(g_a),
        ],
        out_specs=pl.BlockSpec((1, tq, cw), lambda b, i: (b, i, 0)),
        out_shape=jax.ShapeDtypeStruct((bsz, s, cw), BF16),
        scratch_shapes=[
            pltpu.VMEM((s, ckv), BF16),
            pltpu.VMEM((s, d_idx), BF16),
            pltpu.VMEM((ATTN_HEADS, ckv, s), BF16),
            pltpu.VMEM((IDX_HEADS * d_idx, s), BF16),
            pltpu.VMEM((IDX_HEADS, s), F32),
            pltpu.VMEM((s, tq), I32),
            pltpu.VMEM((32, s // 32, tq), I32),
            pltpu.VMEM((ATTN_HEADS, 5 * tq, tq), F32),
            pltpu.VMEM((ATTN_HEADS * tq, ckv), F32),
            pltpu.VMEM((2, ATTN_HEADS, KV_CHUNK, tq), F32),
        ],
        compiler_params=_params(("arbitrary", "arbitrary"), 56),
        name="dsa",
    )(rel_bias, p_aq, p_akv, p_misc, qnw, kvnw, wq, wqi, wuk_t, wuv_h, g_a)


def _out_mlp_kernel(x_ref, yl_ref, ym_ref, ya_ref, mod_ref, wo_ref, n2_ref, w1_ref, w2_ref, fw_ref, out_ref,
                    h2_s, acc_s, *, final):
    j = pl.program_id(2)

    @pl.when(j == 0)
    def _():
        cat = jnp.concatenate([yl_ref[0], ym_ref[0], ya_ref[0]], axis=1)
        x1 = x_ref[0] + mod_ref[2, 0] * _dot(cat, wo_ref[...])
        out_ref[0] = x1
        h2_s[...] = (_rms(x1, n2_ref[...]) * (1.0 + mod_ref[4, 0]) + mod_ref[3, 0]).astype(BF16)
        acc_s[...] = jnp.zeros_like(acc_s)

    a = jnp.maximum(_dot(h2_s[...], w1_ref[...]), 0.0)
    acc_s[...] += _dot((a * a).astype(BF16), w2_ref[...])

    @pl.when(j == pl.num_programs(2) - 1)
    def _():
        x2 = out_ref[0] + mod_ref[5, 0] * acc_s[...]
        if final:
            x2 = _rms(x2, fw_ref[...])
        out_ref[0] = x2


def _out_mlp(x, y_lru, y_m, y_a, mod, wo, n2w, w1, w2, fw, final):
    bsz, s, d = x.shape
    dff = w1.shape[1]
    t = min(MLP_TILE, s)
    fc = min(FF_CHUNK, dff)
    tok = lambda c: pl.BlockSpec((1, t, c), lambda b, i, j: (b, i, 0))
    return pl.pallas_call(
        functools.partial(_out_mlp_kernel, final=final),
        grid=(bsz, s // t, dff // fc),
        in_specs=[
            tok(d), tok(y_lru.shape[2]), tok(y_m.shape[2]), tok(y_a.shape[2]),
            pl.BlockSpec((N_ADA, 1, 1, d), lambda b, i, j: (0, b, 0, 0)),
            pl.BlockSpec((d, d), lambda b, i, j: (0, 0)),
            pl.BlockSpec((1, d), lambda b, i, j: (0, 0)),
            pl.BlockSpec((d, fc), lambda b, i, j: (0, j)),
            pl.BlockSpec((fc, d), lambda b, i, j: (j, 0)),
            pl.BlockSpec((1, d), lambda b, i, j: (0, 0)),
        ],
        out_specs=tok(d),
        out_shape=jax.ShapeDtypeStruct((bsz, s, d), F32),
        scratch_shapes=[pltpu.VMEM((t, d), BF16), pltpu.VMEM((t, d), F32)],
        compiler_params=_params(("parallel", "parallel", "arbitrary"), 56),
        name="out_mlp",
    )(x, y_lru, y_m, y_a, mod, wo, n2w, w1, w2, fw)


def _block_diag(w):
    nb, bi, bo = w.shape
    out = jnp.zeros((nb * bi, nb * bo), w.dtype)
    for n in range(nb):
        out = out.at[n * bi:(n + 1) * bi, n * bo:(n + 1) * bo].set(w[n])
    return out


def kernel(x, c, w_in, conv_w, conv_b, lru_wa, lru_ba, lru_wx, lru_bx, lru_lambda, mlstm_bi, mlstm_bf, w_q_up,
           w_qidx_up, w_uk, w_uv, q_lat_norm_w, kv_lat_norm_w, rel_bias, group_norm_w, w_o, w_ada, b_ada, norm1_w,
           norm2_w, w_mlp1, w_mlp2, final_norm_w):
    depth, d, _ = w_in.shape
    bsz = x.shape[0]
    cl = conv_w.shape[2]
    cm = d // 2
    cq = w_q_up.shape[1]
    ckv = w_uk.shape[1]
    d_idx = w_qidx_up.shape[3]
    nh = MLSTM_HEADS
    dims = (cl, cm, cq, ckv)

    mod_all = _ada(c, w_ada, b_ada).reshape(depth, N_ADA, bsz, 1, d)

    o_gate = 2 * cl + 4 * cm
    o_aq = o_gate + 2 * nh
    o_akv = o_aq + cq
    o_ik = o_akv + ckv
    o_iw = o_ik + d_idx
    zeros = lambda n: jnp.zeros((depth, d, n), w_in.dtype)
    w_in_p = jnp.concatenate([
        w_in[:, :, :o_gate], w_in[:, :, o_aq:o_akv], w_in[:, :, o_akv:o_ik],
        w_in[:, :, o_gate:o_aq], w_in[:, :, o_iw:o_iw + IDX_HEADS], zeros(32 - 2 * nh - IDX_HEADS),
        w_in[:, :, o_ik:o_iw], zeros(LANE - 32 - d_idx),
    ], axis=2).astype(BF16)

    gate_bias = jnp.concatenate([mlstm_bi, mlstm_bf, jnp.zeros((depth, LANE - 2 * nh), F32)], axis=1)
    fw = final_norm_w.reshape(1, d)

    for l in range(depth):
        mod = mod_all[l]
        p_lru, p_qkv, p_mo, p_aq, p_akv, p_misc = _inproj(x, mod, norm1_w[l].reshape(1, d), w_in_p[l], dims)
        gw = group_norm_w[l]
        y_lru = _lru(p_lru, conv_w[l], conv_b[l].reshape(1, cl),
                     _block_diag(lru_wa[l]).astype(BF16), lru_ba[l].reshape(1, cl),
                     _block_diag(lru_wx[l]).astype(BF16), lru_bx[l].reshape(1, cl),
                     lru_lambda[l].reshape(1, cl), gw[:cl].reshape(1, cl))
        y_m = _mlstm(p_qkv, p_mo, p_misc, gate_bias[l].reshape(1, LANE), gw[cl:cl + cm].reshape(1, cm))
        wq = w_q_up[l].reshape(cq, -1).T.astype(BF16)
        wqi = w_qidx_up[l].reshape(cq, -1).T.astype(BF16)
        wuk_t = jnp.transpose(w_uk[l], (1, 0, 2)).astype(BF16)
        wuv_h = jnp.transpose(w_uv[l], (1, 0, 2)).astype(BF16)
        y_a = _dsa(rel_bias, p_aq, p_akv, p_misc, q_lat_norm_w[l].reshape(1, cq), kv_lat_norm_w[l].reshape(1, ckv),
                   wq, wqi, wuk_t, wuv_h, gw[cl + cm:].reshape(1, -1))
        x = _out_mlp(x, y_lru, y_m, y_a, mod, w_o[l].astype(BF16), norm2_w[l].reshape(1, d),
                     w_mlp1[l].astype(BF16), w_mlp2[l].astype(BF16), fw, final=(l == depth - 1))
    return x
```

```python
import functools
import math

import jax
import jax.numpy as jnp
from jax import lax
from jax.experimental import pallas as pl
from jax.experimental.pallas import tpu as pltpu

F32 = jnp.float32
BF16 = jnp.bfloat16
I32 = jnp.int32

NORM_EPS = 1e-6
CONV_WIDTH = 4
LRU_C = 8.0
MLSTM_HEADS = 4
GATE_SOFTCAP = 15.0
ATTN_HEADS = 4
IDX_HEADS = 8
INDEX_TOPK = 256
REL_BUCKETS = 32
REL_MAX_EXACT = 16
REL_MAX_DIST = 128
N_ADA = 6

LANE = 128
Q_BLOCK = 256
KV_CHUNK = 2 * Q_BLOCK
BIAS_ROWS = 64
WORD_ROWS = 256
SCORE_ROWS = 128
PROJ_COLS = 1024
MLSTM_CHUNK = 256
MLSTM_STEP_CHUNKS = 4
TOKEN_TILE = 512
LRU_STEP_TILES = 4
MLP_TILE = 512
FF_CHUNK = 1024
INT_MIN = -(2 ** 31)
INT_MAX = 2 ** 31 - 1
NEG_BIG = -1e30
M_INIT = -1e29
LOG2E = math.log2(math.e)
MIB = 1024 * 1024


def _rms(x, w):
    return x * lax.rsqrt(jnp.mean(x * x, axis=-1, keepdims=True) + NORM_EPS) * w


def _sigmoid(x):
    return 1.0 / (1.0 + jnp.exp(-x))


def _dot(a, b):
    return jnp.dot(a, b, preferred_element_type=F32)


def _dot_nt(a, b):
    return lax.dot_general(a, b, (((1,), (1,)), ((), ())), preferred_element_type=F32)


def _bit_transpose32(a):
    a = list(a)
    mask = 0x0000FFFF
    j = 16
    while j:
        k = 0
        while k < 32:
            t = (a[k] ^ lax.shift_right_logical(a[k | j], j)) & mask
            a[k] = a[k] ^ t
            a[k | j] = a[k | j] ^ lax.shift_left(t, j)
            k = ((k | j) + 1) & ~j
        j >>= 1
        mask = (mask ^ (mask << j)) & 0xFFFFFFFF
    return a


def _params(sem, vmem_mib):
    return pltpu.CompilerParams(dimension_semantics=sem, vmem_limit_bytes=vmem_mib * MIB)


def _ada_kernel(c_ref, w_ref, b_ref, o_ref):
    c = c_ref[...]
    act = (c * _sigmoid(c)).astype(BF16)
    o_ref[0, 0] = _dot(act, w_ref[0].astype(BF16)) + b_ref[0, 0]


def _ada(c, w_ada, b_ada):
    depth, d, _ = w_ada.shape
    bsz = c.shape[0]
    b4 = b_ada.reshape(depth, N_ADA, 1, d)
    return pl.pallas_call(
        _ada_kernel,
        grid=(depth, N_ADA),
        in_specs=[
            pl.BlockSpec((bsz, d), lambda l, k: (0, 0)),
            pl.BlockSpec((1, d, d), lambda l, k: (l, 0, k)),
            pl.BlockSpec((1, 1, 1, d), lambda l, k: (l, k, 0, 0)),
        ],
        out_specs=pl.BlockSpec((1, 1, bsz, d), lambda l, k: (l, k, 0, 0)),
        out_shape=jax.ShapeDtypeStruct((depth, N_ADA, bsz, d), F32),
        compiler_params=_params(("arbitrary", "arbitrary"), 32),
        name="ada_mod",
    )(c, w_ada, b4)


def _inproj_kernel(x_ref, mod_ref, nw_ref, w_ref, lru_ref, qkv_ref, mo_ref, aq_ref, akv_ref, misc_ref, *, dims):
    cl, cm, cq, ckv = dims
    x = x_ref[0]
    sh = mod_ref[0, 0]
    sc = mod_ref[1, 0]
    h = _rms(x, nw_ref[...]) * (1.0 + sc) + sh
    proj = _dot(h.astype(BF16), w_ref[...])
    o = 0
    lru_ref[0] = proj[:, o:o + 2 * cl]
    o += 2 * cl
    qkv_ref[0] = proj[:, o:o + 3 * cm].astype(BF16)
    o += 3 * cm
    mo_ref[0] = proj[:, o:o + cm]
    o += cm
    aq_ref[0] = proj[:, o:o + cq]
    o += cq
    akv_ref[0] = proj[:, o:o + ckv]
    o += ckv
    misc_ref[0] = proj[:, o:o + LANE]


def _inproj(x, mod, nw, w_in_p, dims):
    bsz, s, d = x.shape
    cl, cm, cq, ckv = dims
    n_out = w_in_p.shape[1]
    t = min(TOKEN_TILE, s)
    tok = lambda c: pl.BlockSpec((1, t, c), lambda b, i: (b, i, 0))
    return pl.pallas_call(
        functools.partial(_inproj_kernel, dims=dims),
        grid=(bsz, s // t),
        in_specs=[
            tok(d),
            pl.BlockSpec((N_ADA, 1, 1, d), lambda b, i: (0, b, 0, 0)),
            pl.BlockSpec((1, d), lambda b, i: (0, 0)),
            pl.BlockSpec((d, n_out), lambda b, i: (0, 0)),
        ],
        out_specs=[tok(2 * cl), tok(3 * cm), tok(cm), tok(cq), tok(ckv), tok(LANE)],
        out_shape=[
            jax.ShapeDtypeStruct((bsz, s, 2 * cl), F32),
            jax.ShapeDtypeStruct((bsz, s, 3 * cm), BF16),
            jax.ShapeDtypeStruct((bsz, s, cm), F32),
            jax.ShapeDtypeStruct((bsz, s, cq), F32),
            jax.ShapeDtypeStruct((bsz, s, ckv), F32),
            jax.ShapeDtypeStruct((bsz, s, LANE), F32),
        ],
        compiler_params=_params(("parallel", "parallel"), 56),
        name="in_proj",
    )(x, mod, nw, w_in_p)


def _lru_kernel(p_ref, cw_ref, cb_ref, wa_ref, ba_ref, wx_ref, bx_ref, lam_ref, g_ref, out_ref, hc_ref, xt_ref):
    t_len = min(TOKEN_TILE, p_ref.shape[1])

    @pl.when(pl.program_id(1) == 0)
    def _():
        hc_ref[...] = jnp.zeros_like(hc_ref)
        xt_ref[...] = jnp.zeros_like(xt_ref)

    lax.fori_loop(0, p_ref.shape[1] // t_len,
                  functools.partial(_lru_tile, p_ref, cw_ref, cb_ref, wa_ref, ba_ref, wx_ref, bx_ref, lam_ref, g_ref,
                                    out_ref, hc_ref, xt_ref, t_len), 0)


def _lru_tile(p_ref, cw_ref, cb_ref, wa_ref, ba_ref, wx_ref, bx_ref, lam_ref, g_ref, out_ref, hc_ref, xt_ref, t_len,
              ti, carry):
    c = p_ref.shape[2] // 2
    rows = pl.ds(pl.multiple_of(ti * t_len, t_len), t_len)
    p = p_ref[0, rows, :]
    x = p[:, :c]
    y = p[:, c:]
    cw = cw_ref[...]
    xe = jnp.concatenate([xt_ref[...], x], axis=0)
    xc = cb_ref[...] + cw[CONV_WIDTH - 1:CONV_WIDTH] * x
    for j in range(1, CONV_WIDTH):
        xc = xc + cw[CONV_WIDTH - 1 - j:CONV_WIDTH - j] * pltpu.roll(xe, j, axis=0)[8:]
    xt_ref[...] = x[t_len - 8:]

    xb = xc.astype(BF16)
    r = _sigmoid(_dot(xb, wa_ref[...]) + ba_ref[...])
    gi = _sigmoid(_dot(xb, wx_ref[...]) + bx_ref[...])
    nl = -lam_ref[...]
    softplus = jnp.maximum(nl, 0.0) + jnp.log(1.0 + jnp.exp(-jnp.abs(nl)))
    log_a = -LRU_C * r * softplus
    a = jnp.exp(log_a)
    u = jnp.sqrt(1.0 - jnp.exp(2.0 * log_a)) * (gi * xc)

    row = lax.broadcasted_iota(I32, (t_len, c), 0)
    d = 1
    while d < t_len:
        keep = row >= d
        a_s = jnp.where(keep, pltpu.roll(a, d, axis=0), 1.0)
        u_s = jnp.where(keep, pltpu.roll(u, d, axis=0), 0.0)
        u = a * u_s + u
        a = a * a_s
        d *= 2
    h = a * hc_ref[...] + u
    hc_ref[...] = h[t_len - 1:]

    gelu = 0.5 * y * (1.0 + jnp.tanh(math.sqrt(2.0 / math.pi) * (y + 0.044715 * (y * y * y))))
    out_ref[0, rows, :] = _rms(gelu * h, g_ref[...]).astype(BF16)
    return carry


def _lru(p_lru, conv_w, conv_b, wa_bd, ba, wx_bd, bx, lam, g_lru):
    bsz, s, c2 = p_lru.shape
    c = c2 // 2
    t = min(TOKEN_TILE * LRU_STEP_TILES, s)
    row = lambda n: pl.BlockSpec((1, n), lambda b, i: (0, 0))
    full = lambda a: pl.BlockSpec(a.shape, lambda b, i: (0, 0))
    return pl.pallas_call(
        _lru_kernel,
        grid=(bsz, s // t),
        in_specs=[
            pl.BlockSpec((1, t, c2), lambda b, i: (b, i, 0)),
            full(conv_w), row(c), full(wa_bd), row(c), full(wx_bd), row(c), row(c), row(c),
        ],
        out_specs=pl.BlockSpec((1, t, c), lambda b, i: (b, i, 0)),
        out_shape=jax.ShapeDtypeStruct((bsz, s, c), BF16),
        scratch_shapes=[pltpu.VMEM((1, c), F32), pltpu.VMEM((8, c), F32)],
        compiler_params=_params(("arbitrary", "arbitrary"), 32),
        name="rg_lru",
    )(p_lru, conv_w, conv_b, wa_bd, ba, wx_bd, bx, lam, g_lru)


def _mlstm_kernel(qkv_ref, mo_ref, misc_ref, gb_ref, nw_ref, out_ref, ct_ref, m_ref):
    l_len = min(MLSTM_CHUNK, qkv_ref.shape[1])

    @pl.when(pl.program_id(1) == 0)
    def _():
        ct_ref[...] = jnp.zeros_like(ct_ref)
        m_ref[...] = jnp.zeros_like(m_ref)

    lax.fori_loop(0, qkv_ref.shape[1] // l_len,
                  functools.partial(_mlstm_chunk, qkv_ref, mo_ref, misc_ref, gb_ref, nw_ref, out_ref, ct_ref, m_ref,
                                    l_len), 0)


def _mlstm_chunk(qkv_ref, mo_ref, misc_ref, gb_ref, nw_ref, out_ref, ct_ref, m_ref, l_len, ci, carry):
    dh = ct_ref.shape[2]
    nh = ct_ref.shape[0]
    rows = pl.ds(pl.multiple_of(ci * l_len, l_len), l_len)
    g = misc_ref[0, rows, :] + gb_ref[...]
    capped = GATE_SOFTCAP * jnp.tanh(g / GATE_SOFTCAP)
    logf = jnp.minimum(capped, 0.0) - jnp.log(1.0 + jnp.exp(-jnp.abs(capped)))
    row = lax.broadcasted_iota(I32, (l_len, LANE), 0)
    bcum = logf
    d = 1
    while d < l_len:
        bcum = bcum + jnp.where(row >= d, pltpu.roll(bcum, d, axis=0), 0.0)
        d *= 2
    capped_t = capped.T
    bcum_t = bcum.T
    causal = (lax.broadcasted_iota(I32, (l_len, l_len), 0) <= lax.broadcasted_iota(I32, (l_len, l_len), 1))
    ones_rows = jnp.ones((8, l_len), F32)

    for h in range(nh):
        q = qkv_ref[0, rows, h * dh:(h + 1) * dh]
        k = qkv_ref[0, rows, (nh + h) * dh:(nh + h + 1) * dh]
        v = qkv_ref[0, rows, (2 * nh + h) * dh:(2 * nh + h + 1) * dh]
        qs = (q.astype(F32) * (dh ** -0.5)).astype(BF16)
        b_row = bcum_t[nh + h:nh + h + 1, :]
        i_row = capped_t[h:h + 1, :]
        r_col = capped[:, h:h + 1] - bcum[:, nh + h:nh + h + 1]
        m_prev = m_ref[h][:, 0:1]
        ct_old = ct_ref[h]

        dmat = jnp.where(causal, r_col + b_row, -jnp.inf)
        inter = b_row + m_prev
        m_row = jnp.maximum(inter, jnp.max(dmat, axis=0, keepdims=True))
        inter_w = jnp.exp(inter - m_row)
        smat = _dot_nt(k, qs) * jnp.exp(dmat - m_row)
        v_t = v.astype(F32).T
        cq = _dot_nt(ct_old.astype(BF16), qs)
        num = _dot(v_t.astype(BF16), smat.astype(BF16)) + inter_w * cq[:dh]
        den = jnp.sum(smat, axis=0, keepdims=True) + inter_w * cq[dh:dh + 1]
        hh = num / jnp.maximum(jnp.abs(den), jnp.exp(-m_row))

        b_last = b_row[:, l_len - 1:]
        w_row = b_last - b_row + i_row
        m_new = jnp.maximum(b_last + m_prev, jnp.max(w_row, axis=1, keepdims=True))
        decay = jnp.exp(b_last + m_prev - m_new)
        vw = jnp.concatenate([v_t, ones_rows], axis=0) * jnp.exp(w_row - m_new)
        ct_ref[h] = decay * ct_old + _dot(vw.astype(BF16), k)
        m_ref[h] = jnp.broadcast_to(m_new, (1, LANE))

        hn_t = hh * lax.rsqrt(jnp.mean(hh * hh, axis=0, keepdims=True) + NORM_EPS)
        hn = hn_t.T * nw_ref[:, h * dh:(h + 1) * dh]
        gate = _sigmoid(mo_ref[0, rows, h * dh:(h + 1) * dh])
        out_ref[0, rows, h * dh:(h + 1) * dh] = (gate * hn).astype(BF16)
    return carry


def _mlstm(p_qkv, p_mo, p_misc, gate_bias, g_m):
    bsz, s, cm = p_mo.shape
    nh = MLSTM_HEADS
    dh = cm // nh
    l_len = min(MLSTM_CHUNK * MLSTM_STEP_CHUNKS, s)
    tok = lambda c: pl.BlockSpec((1, l_len, c), lambda b, i: (b, i, 0))
    return pl.pallas_call(
        _mlstm_kernel,
        grid=(bsz, s // l_len),
        in_specs=[
            tok(3 * cm), tok(cm), tok(LANE),
            pl.BlockSpec((1, LANE), lambda b, i: (0, 0)),
            pl.BlockSpec((1, cm), lambda b, i: (0, 0)),
        ],
        out_specs=tok(cm),
        out_shape=jax.ShapeDtypeStruct((bsz, s, cm), BF16),
        scratch_shapes=[pltpu.VMEM((nh, dh + 8, dh), F32), pltpu.VMEM((nh, 1, LANE), F32)],
        compiler_params=_params(("arbitrary", "arbitrary"), 32),
        name="mlstm",
    )(p_qkv, p_mo, p_misc, gate_bias, g_m)


def _dsa_kernel(rb_ref, aq_ref, akv_ref, misc_ref, qnw_ref, kvnw_ref, wq_ref, wqi_ref, wuk_ref, wuv_ref,
                g_ref, out_ref, ckv_s, ik_s, qa_s, qidx_s, wi_s, key_s, bits_s, nb_s, acc_s, lg_s, *, topk):
    qi = pl.program_id(1)
    tq = out_ref.shape[1]
    n_heads = wuk_ref.shape[0]
    dh = wuk_ref.shape[2]
    d_idx = ik_s.shape[1]

    @pl.when(qi == 0)
    def _():
        ckv_s[...] = _rms(akv_ref[0], kvnw_ref[...]).astype(BF16)
        ik_s[...] = misc_ref[0][:, 32:32 + d_idx].astype(BF16)
        bits_s[...] = jnp.zeros_like(bits_s)

        slab = min(PROJ_COLS, aq_ref.shape[1])

        def project(ci, carry):
            rows = pl.ds(pl.multiple_of(ci * slab, slab), slab)
            q_lat_t = _rms(aq_ref[0, rows, :], qnw_ref[...]).T.astype(BF16)
            q_t = _dot(wq_ref[...], q_lat_t).astype(BF16)
            for h in range(n_heads):
                qa_s[h, :, rows] = (_dot(wuk_ref[h], q_t[h * dh:(h + 1) * dh]) * (dh ** -0.5 * LOG2E)).astype(BF16)
            qidx_s[:, rows] = (_dot(wqi_ref[...], q_lat_t) * (d_idx ** -0.5)).astype(BF16)
            wi_s[:, rows] = misc_ref[0, rows, :].T[8:8 + IDX_HEADS] * (IDX_HEADS ** -0.5)
            return carry

        lax.fori_loop(0, aq_ref.shape[1] // slab, project, 0)

    @pl.when((pl.program_id(0) == 0) & (qi == 0))
    def _():
        def build(rb, carry):
            r0 = pl.multiple_of(rb * BIAS_ROWS, BIAS_ROWS)
            rr = lax.broadcasted_iota(I32, (BIAS_ROWS, tq), 0) + r0
            ii = lax.broadcasted_iota(I32, (BIAS_ROWS, tq), 1)
            n = jnp.maximum(ii + 3 * tq - rr, 0)
            log_ratio = jnp.log(jnp.maximum(n, 1).astype(F32) / REL_MAX_EXACT) / math.log(REL_MAX_DIST / REL_MAX_EXACT)
            large = jnp.minimum(REL_MAX_EXACT + (log_ratio * (REL_BUCKETS - REL_MAX_EXACT)).astype(I32),
                                REL_BUCKETS - 1)
            bucket = jnp.where(n < REL_MAX_EXACT, n, large)
            for h in range(n_heads):
                last = rb_ref[REL_BUCKETS - 1, h]
                val = jnp.zeros((BIAS_ROWS, tq), F32)
                for kb in range(REL_BUCKETS - 1):
                    val = jnp.where(bucket == kb, (rb_ref[kb, h] - last) * LOG2E, val)
                nb_s[h, pl.ds(r0, BIAS_ROWS), :] = val
            return carry

        lax.fori_loop(0, nb_s.shape[1] // BIAS_ROWS, build, 0)

    cols = pl.ds(pl.multiple_of(qi * tq, tq), tq)
    qa_all_t = jnp.concatenate([qa_s[h, :, cols] for h in range(n_heads)], axis=1)
    q_idx_all = jnp.concatenate([qidx_s[h * d_idx:(h + 1) * d_idx, cols] for h in range(IDX_HEADS)], axis=1)
    w_i = wi_s[:, cols]

    t_pos = qi * tq + lax.broadcasted_iota(I32, (SCORE_ROWS, tq), 1)
    row = lax.broadcasted_iota(I32, (SCORE_ROWS, tq), 0)

    def score_group(g):
        for sub in range(WORD_ROWS // SCORE_ROWS):
            r0 = pl.multiple_of(g * WORD_ROWS + sub * SCORE_ROWS, SCORE_ROWS)
            d_all = _dot(ik_s[pl.ds(r0, SCORE_ROWS), :], q_idx_all)
            sc = jnp.zeros((SCORE_ROWS, tq), F32)
            for h in range(IDX_HEADS):
                sc = sc + jnp.maximum(d_all[:, h * tq:(h + 1) * tq], 0.0) * w_i[h:h + 1]
            bits = lax.bitcast_convert_type(sc, I32)
            key = jnp.where(bits < 0, bits ^ INT_MAX, bits)
            key_s[pl.ds(r0, SCORE_ROWS), :] = jnp.where(row + r0 <= t_pos, key, INT_MIN)

    lane_tiles = tq // LANE

    def slice_group(g):
        k0 = pl.multiple_of(g * WORD_ROWS, WORD_ROWS)
        w0 = pl.multiple_of(g * 8, 8)
        for lt in range(lane_tiles):
            cols = slice(lt * LANE, (lt + 1) * LANE)
            w = _bit_transpose32([key_s[pl.ds(k0 + j * 8, 8), cols] ^ INT_MIN for j in range(32)])
            for i in range(32):
                bits_s[31 - i, pl.ds(w0, 8), cols] = w[i]

    def score_and_slice(g, carry):
        slice_group(g - 1)
        score_group(g)
        return carry

    n_word_groups = (qi + 1) * (tq // WORD_ROWS)

    @pl.when((qi & 1) == 0)
    def _():
        key_s[pl.ds(pl.multiple_of((qi + 1) * tq, tq), tq), :] = jnp.full((tq, tq), INT_MIN, I32)

    score_group(0)
    lax.fori_loop(1, n_word_groups, score_and_slice, 0)
    slice_group(n_word_groups - 1)

    n_word_rows = bits_s.shape[1]
    wrow = lax.broadcasted_iota(I32, (n_word_rows, tq), 0)
    first_key = lax.shift_right_logical(wrow, 3) * WORD_ROWS + (wrow & 7)
    t_lane = qi * tq + lax.broadcasted_iota(I32, (n_word_rows, tq), 1)
    n_valid = jnp.clip(lax.shift_right_arithmetic(t_lane - first_key, 3) + 1, 0, 32)
    eq0 = jnp.where(n_valid <= 0, 0, lax.shift_left(jnp.int32(-1), 32 - jnp.maximum(n_valid, 1)))

    def search(rows):
        def bit_step(i, carry):
            eq, c_gt, ans = carry
            b = 31 - i
            t = eq & bits_s[b, :rows]
            cnt = jnp.sum(lax.population_count(t), axis=0, keepdims=True)
            take = (c_gt + cnt) >= topk
            eq = jnp.where(take, t, eq ^ t)
            c_gt = jnp.where(take, c_gt, c_gt + cnt)
            ans = jnp.where(take, ans | lax.shift_left(jnp.int32(1), b), ans)
            return eq, c_gt, ans

        zero_row = jnp.zeros((1, tq), I32)
        return lax.fori_loop(0, 32, bit_step, (eq0[:rows], zero_row, zero_row))[2]

    quarter = pl.cdiv(n_word_rows, 32) * 8
    used_rows = n_word_groups * 8
    ans = lax.cond(
        used_rows <= 2 * quarter,
        lambda: lax.cond(used_rows <= quarter, lambda: search(quarter), lambda: search(2 * quarter)),
        lambda: lax.cond(used_rows <= 3 * quarter, lambda: search(3 * quarter), lambda: search(n_word_rows)))
    thr = jnp.maximum(ans ^ INT_MIN, INT_MIN + 1)

    c_last = lax.shift_right_logical(qi, 1)
    acc_s[...] = jnp.zeros_like(acc_s)

    def masked_logits(c, slot):
        k0 = pl.multiple_of(c * KV_CHUNK, KV_CHUNK)
        sel = key_s[pl.ds(k0, KV_CHUNK), :] >= thr
        logits_all = _dot(ckv_s[pl.ds(k0, KV_CHUNK), :], qa_all_t)
        b0 = pl.multiple_of(jnp.maximum(2 * c - qi + 3, 0) * tq, tq)
        mx = []
        for h in range(n_heads):
            logits = logits_all[:, h * tq:(h + 1) * tq] + nb_s[h, pl.ds(b0, KV_CHUNK), :]
            logits = jnp.where(sel, logits, NEG_BIG)
            lg_s[slot, h] = logits
            mx.append(jnp.max(logits, axis=0, keepdims=True))
        return tuple(mx)

    def per_query_rows(rows):
        stacked = jnp.concatenate(rows, axis=1)
        return jnp.broadcast_to(stacked, (acc_s.shape[1], stacked.shape[1])).T

    def softmax_update(c, slot, mx, m_old, l_old):
        k0 = pl.multiple_of(c * KV_CHUNK, KV_CHUNK)
        m_out, l_out, alphas, ps = [], [], [], []
        for h in range(n_heads):
            m_new = jnp.maximum(m_old[h], mx[h])
            alpha = jnp.exp2(m_old[h] - m_new)
            p = jnp.exp2(lg_s[slot, h] - m_new)
            l_out.append(alpha * l_old[h] + jnp.sum(p, axis=0, keepdims=True))
            m_out.append(m_new)
            alphas.append(alpha)
            ps.append(p.astype(BF16))
        pv = lax.dot_general(jnp.concatenate(ps, axis=1), ckv_s[pl.ds(k0, KV_CHUNK), :], (((0,), (0,)), ((), ())),
                             preferred_element_type=F32)
        acc_s[...] = per_query_rows(alphas) * acc_s[...] + pv
        return tuple(m_out), tuple(l_out)

    def attend(c, carry):
        mx, m_old, l_old = carry
        m_new, l_new = softmax_update(c, c & 1, mx, m_old, l_old)
        return masked_logits(c + 1, (c + 1) & 1), m_new, l_new

    carry = (masked_logits(0, 0),
             tuple(jnp.full((1, tq), M_INIT, F32) for _ in range(n_heads)),
             tuple(jnp.zeros((1, tq), F32) for _ in range(n_heads)))
    carry = lax.fori_loop(0, c_last, attend, carry)
    _, l_fin = softmax_update(c_last, c_last & 1, *carry)
    o_lat = (acc_s[...] / per_query_rows(l_fin)).astype(BF16)
    o = jnp.concatenate([_dot(o_lat[h * tq:(h + 1) * tq], wuv_ref[h]) for h in range(n_heads)], axis=1)
    out_ref[0] = _rms(o, g_ref[...]).astype(BF16)


def _dsa(rel_bias, p_aq, p_akv, p_misc, qnw, kvnw, wq, wqi, wuk_t, wuv_h, g_a):
    bsz, s, cq = p_aq.shape
    ckv = p_akv.shape[2]
    d_idx = wqi.shape[0] // IDX_HEADS
    topk = min(INDEX_TOPK, s // 4)
    tq = Q_BLOCK
    assert s % KV_CHUNK == 0 and tq % WORD_ROWS == 0 and WORD_ROWS % SCORE_ROWS == 0
    cw = wq.shape[0]
    full2 = lambda a: pl.BlockSpec(a.shape, lambda b, i: (0, 0))
    full3 = lambda a: pl.BlockSpec(a.shape, lambda b, i: (0, 0, 0))
    return pl.pallas_call(
        functools.partial(_dsa_kernel, topk=topk),
        grid=(bsz, s // tq),
        in_specs=[
            pl.BlockSpec(memory_space=pltpu.SMEM),
            pl.BlockSpec((1, s, cq), lambda b, i: (b, 0, 0)),
            pl.BlockSpec((1, s, ckv), lambda b, i: (b, 0, 0)),
            pl.BlockSpec((1, s, LANE), lambda b, i: (b, 0, 0)),
            full2(qnw), full2(kvnw), full2(wq), full2(wqi), full3(wuk_t), full3(wuv_h), full2(g_a),
        ],
        out_specs=pl.BlockSpec((1, tq, cw), lambda b, i: (b, i, 0)),
        out_shape=jax.ShapeDtypeStruct((bsz, s, cw), BF16),
        scratch_shapes=[
            pltpu.VMEM((s, ckv), BF16),
            pltpu.VMEM((s, d_idx), BF16),
            pltpu.VMEM((ATTN_HEADS, ckv, s), BF16),
            pltpu.VMEM((IDX_HEADS * d_idx, s), BF16),
            pltpu.VMEM((IDX_HEADS, s), F32),
            pltpu.VMEM((s, tq), I32),
            pltpu.VMEM((32, s // 32, tq), I32),
            pltpu.VMEM((ATTN_HEADS, 5 * tq, tq), F32),
            pltpu.VMEM((ATTN_HEADS * tq, ckv), F32),
            pltpu.VMEM((2, ATTN_HEADS, KV_CHUNK, tq), F32),
        ],
        compiler_params=_params(("arbitrary", "arbitrary"), 56),
        name="dsa",
    )(rel_bias, p_aq, p_akv, p_misc, qnw, kvnw, wq, wqi, wuk_t, wuv_h, g_a)


def _out_mlp_kernel(x_ref, yl_ref, ym_ref, ya_ref, mod_ref, wo_ref, n2_ref, w1_ref, w2_ref, fw_ref, out_ref,
                    *, final):
    cat = jnp.concatenate([yl_ref[0], ym_ref[0], ya_ref[0]], axis=1)
    x1 = x_ref[0] + mod_ref[2, 0] * _dot(cat, wo_ref[...])
    h2 = (_rms(x1, n2_ref[...]) * (1.0 + mod_ref[4, 0]) + mod_ref[3, 0]).astype(BF16)
    dff = w1_ref.shape[1]
    fc = min(FF_CHUNK, dff)
    acc = jnp.zeros_like(x1)
    for j in range(dff // fc):
        a = jnp.maximum(_dot(h2, w1_ref[:, j * fc:(j + 1) * fc]), 0.0)
        acc = acc + _dot((a * a).astype(BF16), w2_ref[j * fc:(j + 1) * fc, :])
    x2 = x1 + mod_ref[5, 0] * acc
    if final:
        x2 = _rms(x2, fw_ref[...])
    out_ref[0] = x2


def _out_mlp(x, y_lru, y_m, y_a, mod, wo, n2w, w1, w2, fw, final):
    bsz, s, d = x.shape
    dff = w1.shape[1]
    t = min(MLP_TILE, s)
    tok = lambda c: pl.BlockSpec((1, t, c), lambda b, i: (b, i, 0))
    resident = lambda a: pl.BlockSpec(a.shape, lambda b, i: (0, 0), pipeline_mode=pl.Buffered(1))
    return pl.pallas_call(
        functools.partial(_out_mlp_kernel, final=final),
        grid=(bsz, s // t),
        in_specs=[
            tok(d), tok(y_lru.shape[2]), tok(y_m.shape[2]), tok(y_a.shape[2]),
            pl.BlockSpec((N_ADA, 1, 1, d), lambda b, i: (0, b, 0, 0)),
            resident(wo),
            pl.BlockSpec((1, d), lambda b, i: (0, 0)),
            resident(w1),
            resident(w2),
            pl.BlockSpec((1, d), lambda b, i: (0, 0)),
        ],
        out_specs=tok(d),
        out_shape=jax.ShapeDtypeStruct((bsz, s, d), F32),
        compiler_params=_params(("parallel", "parallel"), 56),
        name="out_mlp",
    )(x, y_lru, y_m, y_a, mod, wo, n2w, w1, w2, fw)


def _block_diag(w):
    nb, bi, bo = w.shape
    out = jnp.zeros((nb * bi, nb * bo), w.dtype)
    for n in range(nb):
        out = out.at[n * bi:(n + 1) * bi, n * bo:(n + 1) * bo].set(w[n])
    return out


def kernel(x, c, w_in, conv_w, conv_b, lru_wa, lru_ba, lru_wx, lru_bx, lru_lambda, mlstm_bi, mlstm_bf, w_q_up,
           w_qidx_up, w_uk, w_uv, q_lat_norm_w, kv_lat_norm_w, rel_bias, group_norm_w, w_o, w_ada, b_ada, norm1_w,
           norm2_w, w_mlp1, w_mlp2, final_norm_w):
    depth, d, _ = w_in.shape
    bsz = x.shape[0]
    cl = conv_w.shape[2]
    cm = d // 2
    cq = w_q_up.shape[1]
    ckv = w_uk.shape[1]
    d_idx = w_qidx_up.shape[3]
    nh = MLSTM_HEADS
    dims = (cl, cm, cq, ckv)

    mod_all = _ada(c, w_ada, b_ada).reshape(depth, N_ADA, bsz, 1, d)

    o_gate = 2 * cl + 4 * cm
    o_aq = o_gate + 2 * nh
    o_akv = o_aq + cq
    o_ik = o_akv + ckv
    o_iw = o_ik + d_idx
    zeros = lambda n: jnp.zeros((depth, d, n), w_in.dtype)
    w_in_p = jnp.concatenate([
        w_in[:, :, :o_gate], w_in[:, :, o_aq:o_akv], w_in[:, :, o_akv:o_ik],
        w_in[:, :, o_gate:o_aq], w_in[:, :, o_iw:o_iw + IDX_HEADS], zeros(32 - 2 * nh - IDX_HEADS),
        w_in[:, :, o_ik:o_iw], zeros(LANE - 32 - d_idx),
    ], axis=2).astype(BF16)

    gate_bias = jnp.concatenate([mlstm_bi, mlstm_bf, jnp.zeros((depth, LANE - 2 * nh), F32)], axis=1)
    fw = final_norm_w.reshape(1, d)

    for l in range(depth):
        mod = mod_all[l]
        p_lru, p_qkv, p_mo, p_aq, p_akv, p_misc = _inproj(x, mod, norm1_w[l].reshape(1, d), w_in_p[l], dims)
        gw = group_norm_w[l]
        y_lru = _lru(p_lru, conv_w[l], conv_b[l].reshape(1, cl),
                     _block_diag(lru_wa[l]).astype(BF16), lru_ba[l].reshape(1, cl),
                     _block_diag(lru_wx[l]).astype(BF16), lru_bx[l].reshape(1, cl),
                     lru_lambda[l].reshape(1, cl), gw[:cl].reshape(1, cl))
        y_m = _mlstm(p_qkv, p_mo, p_misc, gate_bias[l].reshape(1, LANE), gw[cl:cl + cm].reshape(1, cm))
        wq = w_q_up[l].reshape(cq, -1).T.astype(BF16)
        wqi = w_qidx_up[l].reshape(cq, -1).T.astype(BF16)
        wuk_t = jnp.transpose(w_uk[l], (1, 0, 2)).astype(BF16)
        wuv_h = jnp.transpose(w_uv[l], (1, 0, 2)).astype(BF16)
        y_a = _dsa(rel_bias, p_aq, p_akv, p_misc, q_lat_norm_w[l].reshape(1, cq), kv_lat_norm_w[l].reshape(1, ckv),
                   wq, wqi, wuk_t, wuv_h, gw[cl + cm:].reshape(1, -1))
        x = _out_mlp(x, y_lru, y_m, y_a, mod, w_o[l].astype(BF16), norm2_w[l].reshape(1, d),
                     w_mlp1[l].astype(BF16), w_mlp2[l].astype(BF16), fw, final=(l == depth - 1))
    return x
```
